```python
import math
import jax, jax.numpy as jnp
from jax import lax
import numpy as np

D_MODEL = 2048
BATCH = 2
SEQ = 4096
DEPTH = 4

N_META = 16
ATTN_HEADS = 16
ATTN_HEAD_DIM = 64
ATTN_WIDTH = ATTN_HEADS * ATTN_HEAD_DIM
Q_BLOCK = 128
CONV_WIDTH = D_MODEL - ATTN_WIDTH
CONV_K = 3
PROJ_WIDTH = 3 * ATTN_WIDTH + ATTN_HEADS + 3 * CONV_WIDTH
S5_GROUP = 16
S5_GROUPS = D_MODEL // S5_GROUP
S5_STATE = 64
S5_MIN_DECAY = 1e-4
S5_DT_MIN = 1e-3
S5_DT_MAX = 1e-1
FFN_HIDDEN = int(math.ceil(8 * D_MODEL / 3 / 256) * 256)
NORM_EPS = 1e-6
N_EVEN = (DEPTH + 1) // 2
N_ODD = DEPTH // 2

kernel_name = "fox_shortconv_s5_hybrid_trunk"


def rms_norm(x, g):
    xf = x.astype(jnp.float32)
    y = xf * lax.rsqrt(jnp.mean(xf * xf, axis=-1, keepdims=True) + NORM_EPS)
    return (y * g.astype(jnp.float32)).astype(x.dtype)


def fox_attention(q, k, v, log_f):
    L = q.shape[1]
    seq_real = L - N_META
    c = jnp.transpose(jnp.cumsum(log_f, axis=1), (0, 2, 1))
    scale = ATTN_HEAD_DIM ** -0.5
    blocks = [(0, N_META)] + [(N_META + i * Q_BLOCK, Q_BLOCK) for i in range(seq_real // Q_BLOCK)]
    outs = []
    for start, size in blocks:
        end = start + size
        qb = q[:, start:end]
        kb = k[:, :end]
        vb = v[:, :end]
        s = jnp.einsum('bqhd,bkhd->bhqk', qb, kb).astype(jnp.float32) * scale
        s = s + c[:, :, start:end][..., None] - c[:, :, None, :end]
        qpos = jnp.arange(start, end)[:, None]
        kpos = jnp.arange(end)[None, :]
        s = jnp.where(kpos <= qpos, s, -jnp.inf)
        p = jax.nn.softmax(s, axis=-1).astype(v.dtype)
        outs.append(jnp.einsum('bhqk,bkhd->bqhd', p, vb))
    return jnp.concatenate(outs, axis=1)


def attn_conv_mixer(h, w_in, b_f, conv_w, conv_b, w_o):
    Bsz, L, _ = h.shape
    proj = h @ w_in
    cuts = [ATTN_WIDTH, 2 * ATTN_WIDTH, 3 * ATTN_WIDTH, 3 * ATTN_WIDTH + ATTN_HEADS,
            3 * ATTN_WIDTH + ATTN_HEADS + CONV_WIDTH, 3 * ATTN_WIDTH + ATTN_HEADS + 2 * CONV_WIDTH]
    q, k, v, fg, gate_b, gate_c, xc = jnp.split(proj, cuts, axis=-1)
    log_f = jax.nn.log_sigmoid((fg + b_f).astype(jnp.float32))
    hshape = (Bsz, L, ATTN_HEADS, ATTN_HEAD_DIM)
    attn = fox_attention(q.reshape(hshape), k.reshape(hshape), v.reshape(hshape), log_f)
    attn = attn.reshape(Bsz, L, ATTN_WIDTH)
    z = gate_c * xc
    zp = jnp.pad(z, ((0, 0), (CONV_K - 1, 0), (0, 0)))
    conv = sum(conv_w[j] * zp[:, j:j + L] for j in range(CONV_K)) + conv_b
    conv_out = gate_b * conv
    return jnp.concatenate([attn, conv_out], axis=-1) @ w_o


def _complex_affine_combine(e1, e2):
    a1r, a1i, b1r, b1i = e1
    a2r, a2i, b2r, b2i = e2
    ar = a2r * a1r - a2i * a1i
    ai = a2r * a1i + a2i * a1r
    br = a2r * b1r - a2i * b1i + b2r
    bi = a2r * b1i + a2i * b1r + b2i
    return (ar, ai, br, bi)


def s5_mixer(u, a_re, a_im, log_step, b_re, b_im, c_re, c_im, d_skip, w_glu1, w_glu2):
    Bsz, L, _ = u.shape
    f32 = jnp.float32
    lam_re = jnp.minimum(a_re.astype(f32), -S5_MIN_DECAY)
    lam_im = a_im.astype(f32)
    delta = jnp.exp(log_step.astype(f32))[:, None]
    mag = jnp.exp(lam_re * delta)
    ang = lam_im * delta
    lb_re = mag * jnp.cos(ang)
    lb_im = mag * jnp.sin(ang)
    den = lam_re * lam_re + lam_im * lam_im
    nr = lb_re - 1.0
    ni = lb_im
    coef_re = (nr * lam_re + ni * lam_im) / den
    coef_im = (ni * lam_re - nr * lam_im) / den
    br_ = b_re.astype(f32)
    bi_ = b_im.astype(f32)
    bb_re = coef_re[..., None] * br_ - coef_im[..., None] * bi_
    bb_im = coef_re[..., None] * bi_ + coef_im[..., None] * br_
    uf = u.astype(f32)
    ug = uf.reshape(Bsz, L, S5_GROUPS, S5_GROUP)
    bu_re = jnp.einsum('blgh,gph->blgp', ug, bb_re)
    bu_im = jnp.einsum('blgh,gph->blgp', ug, bb_im)
    a_r = jnp.broadcast_to(lb_re, bu_re.shape)
    a_i = jnp.broadcast_to(lb_im, bu_re.shape)
    _, _, x_re, x_im = lax.associative_scan(_complex_affine_combine, (a_r, a_i, bu_re, bu_im), axis=1)
    y = (jnp.einsum('blgp,ghp->blgh', x_re, c_re.astype(f32))
         - jnp.einsum('blgp,ghp->blgh', x_im, c_im.astype(f32)))
    y = y.reshape(Bsz, L, D_MODEL) + d_skip.astype(f32) * uf
    g = jax.nn.gelu(y).astype(u.dtype)
    return (g @ w_glu1) * jax.nn.sigmoid(g @ w_glu2)


def swiglu_ffn(h, w_gate, w_up, w_down):
    return (jax.nn.silu(h @ w_gate) * (h @ w_up)) @ w_down


def setup_inputs(seed: int = 0) -> dict:
    key = jax.random.key(seed)
    ks = jax.random.split(key, 24)
    f32 = jnp.float32
    D = D_MODEL

    def nrm(k, shape, scale):
        return jax.random.normal(k, shape, f32) * scale

    x = nrm(ks[0], (BATCH, SEQ, D), 1.0)
    meta_tokens = nrm(ks[1], (N_META, D), 1.0)
    norm_g = 1.0 + nrm(ks[2], (DEPTH, 4, D), 0.02)
    ab_w_in = nrm(ks[3], (N_EVEN, D, PROJ_WIDTH), D ** -0.5)
    ab_b_f = 3.0 + nrm(ks[4], (N_EVEN, ATTN_HEADS), 0.5)
    ab_conv_w = nrm(ks[5], (N_EVEN, CONV_K, CONV_WIDTH), CONV_K ** -0.5)
    ab_conv_b = nrm(ks[6], (N_EVEN, CONV_WIDTH), 0.02)
    ab_w_o = nrm(ks[7], (N_EVEN, D, D), D ** -0.5)
    s5_a_re = -0.5 + nrm(ks[8], (N_ODD, S5_GROUPS, S5_STATE), 0.01)
    s5_a_im = (jnp.pi * jnp.arange(S5_STATE, dtype=f32))[None, None, :] + nrm(ks[9], (N_ODD, S5_GROUPS, S5_STATE), 0.01)
    s5_log_step = jax.random.uniform(ks[10], (N_ODD, S5_GROUPS), f32, math.log(S5_DT_MIN), math.log(S5_DT_MAX))
    s5_b_re = nrm(ks[11], (N_ODD, S5_GROUPS, S5_STATE, S5_GROUP), (2 * S5_GROUP) ** -0.5)
    s5_b_im = nrm(ks[12], (N_ODD, S5_GROUPS, S5_STATE, S5_GROUP), (2 * S5_GROUP) ** -0.5)
    s5_c_re = nrm(ks[13], (N_ODD, S5_GROUPS, S5_GROUP, S5_STATE), (2 * S5_STATE) ** -0.5)
    s5_c_im = nrm(ks[14], (N_ODD, S5_GROUPS, S5_GROUP, S5_STATE), (2 * S5_STATE) ** -0.5)
    s5_d = nrm(ks[15], (N_ODD, D), 1.0)
    s5_w_glu1 = nrm(ks[16], (N_ODD, D, D), D ** -0.5)
    s5_w_glu2 = nrm(ks[17], (N_ODD, D, D), D ** -0.5)
    ffn_w_gate = nrm(ks[18], (DEPTH, D, FFN_HIDDEN), D ** -0.5)
    ffn_w_up = nrm(ks[19], (DEPTH, D, FFN_HIDDEN), D ** -0.5)
    ffn_w_down = nrm(ks[20], (DEPTH, FFN_HIDDEN, D), FFN_HIDDEN ** -0.5)
    return {"x": x, "meta_tokens": meta_tokens, "norm_g": norm_g,
            "ab_w_in": ab_w_in, "ab_b_f": ab_b_f, "ab_conv_w": ab_conv_w, "ab_conv_b": ab_conv_b, "ab_w_o": ab_w_o,
            "s5_a_re": s5_a_re, "s5_a_im": s5_a_im, "s5_log_step": s5_log_step,
            "s5_b_re": s5_b_re, "s5_b_im": s5_b_im, "s5_c_re": s5_c_re, "s5_c_im": s5_c_im,
            "s5_d": s5_d, "s5_w_glu1": s5_w_glu1, "s5_w_glu2": s5_w_glu2,
            "ffn_w_gate": ffn_w_gate, "ffn_w_up": ffn_w_up, "ffn_w_down": ffn_w_down}


def reference(x, meta_tokens, norm_g, ab_w_in, ab_b_f, ab_conv_w, ab_conv_b, ab_w_o,
              s5_a_re, s5_a_im, s5_log_step, s5_b_re, s5_b_im, s5_c_re, s5_c_im,
              s5_d, s5_w_glu1, s5_w_glu2, ffn_w_gate, ffn_w_up, ffn_w_down):
    Bsz = x.shape[0]
    meta = jnp.broadcast_to(meta_tokens.astype(x.dtype)[None], (Bsz, N_META, D_MODEL))
    h = jnp.concatenate([meta, x], axis=1)
    for i in range(DEPTH):
        u = rms_norm(h, norm_g[i, 0])
        if i % 2 == 0:
            j = i // 2
            m = attn_conv_mixer(u, ab_w_in[j], ab_b_f[j], ab_conv_w[j], ab_conv_b[j], ab_w_o[j])
        else:
            j = i // 2
            m = s5_mixer(u, s5_a_re[j], s5_a_im[j], s5_log_step[j], s5_b_re[j], s5_b_im[j],
                         s5_c_re[j], s5_c_im[j], s5_d[j], s5_w_glu1[j], s5_w_glu2[j])
        h = h + rms_norm(m, norm_g[i, 1])
        f = swiglu_ffn(rms_norm(h, norm_g[i, 2]), ffn_w_gate[i], ffn_w_up[i], ffn_w_down[i])
        h = h + rms_norm(f, norm_g[i, 3])
    return h[:, N_META:]
```

```python
import functools
import math

import jax
import jax.numpy as jnp
from jax import lax
from jax.experimental import pallas as pl
from jax.experimental.pallas import tpu as pltpu

F32 = jnp.float32
BF16 = jnp.bfloat16

N_META = 16
ATTN_HEADS = 16
HEAD_DIM = 64
ATTN_WIDTH = ATTN_HEADS * HEAD_DIM
CONV_K = 3
S5_GROUP = 16
S5_STATE = 64
S5_MIN_DECAY = 1e-4
NORM_EPS = 1e-6

LANES = 128
SUBLANES = 8
ROW_BLOCK = 768
SEQ_BLOCK = 384
HEAD_PAIRS = ATTN_HEADS * HEAD_DIM // LANES
S5_SET_GROUPS = 16
S5_SET_IN = S5_SET_GROUPS * S5_GROUP
S5_SET_STATE = S5_SET_GROUPS * S5_STATE
STATE_TILES = S5_SET_STATE // LANES
SLAB_PITCH = SEQ_BLOCK + SUBLANES
KEY_MASK = -1e30
VMEM_LIMIT = 60 * 1024 * 1024


def _params(sem):
    return pltpu.CompilerParams(dimension_semantics=sem, vmem_limit_bytes=VMEM_LIMIT)


def _rmsnorm_body(x_ref, g_ref, o_ref):
    x = x_ref[...]
    ms = jnp.mean(x * x, axis=-1, keepdims=True)
    o_ref[...] = (x * lax.rsqrt(ms + NORM_EPS) * g_ref[...]).astype(o_ref.dtype)


def _rmsnorm(h, g, out_dtype):
    m, d = h.shape
    return pl.pallas_call(
        _rmsnorm_body,
        grid=(m // ROW_BLOCK,),
        in_specs=[pl.BlockSpec((ROW_BLOCK, d), lambda i: (i, 0)),
                  pl.BlockSpec((1, d), lambda i: (0, 0))],
        out_specs=pl.BlockSpec((ROW_BLOCK, d), lambda i: (i, 0)),
        out_shape=jax.ShapeDtypeStruct((m, d), out_dtype),
        compiler_params=_params(("parallel",)),
        name="rmsnorm",
    )(h, g.reshape(1, d))


def _matmul_slabs_body(x_ref, w_ref, o_ref, *, nslab):
    r = jnp.dot(x_ref[...], w_ref[...], preferred_element_type=F32)
    for c in range(nslab):
        o_ref[c] = r[:, c * LANES:(c + 1) * LANES].astype(o_ref.dtype)


def _matmul_slabs(x, w, out_dtype, tn, name):
    m, k = x.shape
    n = w.shape[1]
    nslab = tn // LANES
    return pl.pallas_call(
        functools.partial(_matmul_slabs_body, nslab=nslab),
        grid=(m // ROW_BLOCK, n // tn),
        in_specs=[pl.BlockSpec((ROW_BLOCK, k), lambda i, j: (i, 0)),
                  pl.BlockSpec((k, tn), lambda i, j: (0, j))],
        out_specs=pl.BlockSpec((nslab, ROW_BLOCK, LANES), lambda i, j: (j, i, 0)),
        out_shape=jax.ShapeDtypeStruct((n // LANES, m, LANES), out_dtype),
        compiler_params=_params(("parallel", "parallel")),
        name=name,
    )(x, w)


def _neg_cum_logf_body(fg_ref, b_ref, o_ref, carry_ref, *, pad):
    c = pl.program_id(1)

    @pl.when(c == 0)
    def _():
        carry_ref[...] = jnp.zeros_like(carry_ref)

    z = fg_ref[...] + b_ref[...]
    logf = jnp.minimum(z, 0.0) - jnp.log1p(jnp.exp(-jnp.abs(z)))
    t = logf.shape[0]
    row = lax.broadcasted_iota(jnp.int32, (t, t), 0)
    col = lax.broadcasted_iota(jnp.int32, (t, t), 1)
    tri = (col <= row).astype(BF16)
    hi = logf.astype(BF16)
    rem = logf - hi.astype(F32)
    mid = rem.astype(BF16)
    lo = (rem - mid.astype(F32)).astype(BF16)
    cs = (jnp.dot(tri, hi, preferred_element_type=F32)
          + jnp.dot(tri, mid, preferred_element_type=F32)
          + jnp.dot(tri, lo, preferred_element_type=F32)) + carry_ref[...]
    carry_ref[...] = cs[t - 1:t, :]
    pos = c * t + lax.broadcasted_iota(jnp.int32, (t, 1), 0)
    o_ref[...] = jnp.where(pos < pad, KEY_MASK, -cs)


def _neg_cum_logf(fg, b_f, batch, lp, pad):
    m = fg.shape[0]
    nblk = lp // SEQ_BLOCK
    return pl.pallas_call(
        functools.partial(_neg_cum_logf_body, pad=pad),
        grid=(batch, nblk),
        in_specs=[pl.BlockSpec((SEQ_BLOCK, LANES), lambda b, c: (b * nblk + c, 0)),
                  pl.BlockSpec((1, LANES), lambda b, c: (0, 0))],
        out_specs=pl.BlockSpec((SEQ_BLOCK, LANES), lambda b, c: (b * nblk + c, 0)),
        out_shape=jax.ShapeDtypeStruct((m, LANES), F32),
        scratch_shapes=[pltpu.VMEM((1, LANES), F32)],
        compiler_params=_params(("parallel", "arbitrary")),
        name="neg_cum_logf",
    )(fg, b_f)


def _fox_attn_body(q_ref, k_ref, v_ref, nc_ref, o_ref, m_ref, l_ref, acc_ref):
    qi = pl.program_id(2)
    ki = pl.program_id(3)
    tq = q_ref.shape[0]
    tk = k_ref.shape[0]
    lane = lax.broadcasted_iota(jnp.int32, (1, LANES), 1)
    first_head = lane < HEAD_DIM

    @pl.when(ki == 0)
    def _():
        m_ref[...] = jnp.full_like(m_ref, -jnp.inf)
        l_ref[...] = jnp.zeros_like(l_ref)
        acc_ref[...] = jnp.zeros_like(acc_ref)

    def update(diagonal):
        q = q_ref[...]
        k = k_ref[...]
        v = v_ref[...]
        alphas = []
        pvs = []
        for hh in range(2):
            sel = first_head if hh == 0 else jnp.logical_not(first_head)
            qh = jnp.where(sel, q, jnp.zeros_like(q)) * (HEAD_DIM ** -0.5)
            s = lax.dot_general(qh, k, (((1,), (1,)), ((), ())), preferred_element_type=F32)
            s = s + nc_ref[hh:hh + 1, :]
            if diagonal:
                row = lax.broadcasted_iota(jnp.int32, (tq, tk), 0)
                col = lax.broadcasted_iota(jnp.int32, (tq, tk), 1)
                s = jnp.where(col <= row, s, -jnp.inf)
            m_prev = m_ref[hh]
            m_new = jnp.maximum(m_prev, jnp.max(s, axis=-1, keepdims=True))
            alpha = jnp.exp(m_prev - m_new)
            p = jnp.exp(s - m_new)
            l_ref[hh] = alpha * l_ref[hh] + jnp.sum(p, axis=-1, keepdims=True)
            m_ref[hh] = m_new
            alphas.append(alpha)
            pvs.append(jnp.dot(p.astype(BF16), v, preferred_element_type=F32))
        alpha = jnp.where(first_head, alphas[0], alphas[1])
        acc_ref[...] = alpha * acc_ref[...] + jnp.where(first_head, pvs[0], pvs[1])

    @pl.when(ki < qi)
    def _():
        update(False)

    @pl.when(ki == qi)
    def _():
        update(True)
        denom = jnp.where(first_head, l_ref[0], l_ref[1])
        o_ref[...] = (acc_ref[...] / denom).astype(o_ref.dtype)


def _fox_attn(qkv, negc, batch, lp):
    m = qkv.shape[1]
    nq = lp // SEQ_BLOCK
    blk = (None, SEQ_BLOCK, LANES)
    return pl.pallas_call(
        _fox_attn_body,
        grid=(batch, HEAD_PAIRS, nq, nq),
        in_specs=[
            pl.BlockSpec(blk, lambda b, p, qi, ki: (p, b * nq + qi, 0)),
            pl.BlockSpec(blk, lambda b, p, qi, ki: (HEAD_PAIRS + p, b * nq + jnp.minimum(ki, qi), 0)),
            pl.BlockSpec(blk, lambda b, p, qi, ki: (2 * HEAD_PAIRS + p, b * nq + jnp.minimum(ki, qi), 0)),
            pl.BlockSpec((None, None, 2, SEQ_BLOCK), lambda b, p, qi, ki: (b, p, 0, jnp.minimum(ki, qi))),
        ],
        out_specs=pl.BlockSpec((SEQ_BLOCK, LANES), lambda b, p, qi, ki: (b * nq + qi, p)),
        out_shape=jax.ShapeDtypeStruct((m, ATTN_WIDTH), BF16),
        scratch_shapes=[pltpu.VMEM((2, SEQ_BLOCK, 1), F32),
                        pltpu.VMEM((2, SEQ_BLOCK, 1), F32),
                        pltpu.VMEM((SEQ_BLOCK, LANES), F32)],
        compiler_params=_params(("parallel", "parallel", "parallel", "arbitrary")),
        name="fox_attn",
    )(qkv, qkv, qkv, negc)


def _gated_conv_body(gb_ref, gc_ref, xc_ref, gch_ref, xch_ref, w_ref, b_ref, o_ref, z_ref):
    i = pl.program_id(1)
    tm = gb_ref.shape[0]
    z = gc_ref[...] * xc_ref[...]
    halo = gch_ref[...] * xch_ref[...]
    z_ref[0:SUBLANES, :] = jnp.where(i > 0, halo, jnp.zeros_like(halo))
    z_ref[SUBLANES:SUBLANES + tm, :] = z
    z1 = z_ref[SUBLANES - 1:SUBLANES - 1 + tm, :]
    z2 = z_ref[SUBLANES - 2:SUBLANES - 2 + tm, :]
    conv = w_ref[0:1, :] * z2 + w_ref[1:2, :] * z1 + w_ref[2:3, :] * z + b_ref[...]
    o_ref[...] = (gb_ref[...] * conv).astype(o_ref.dtype)


def _gated_conv(gates, conv_w, conv_b):
    nch = gates.shape[0] // 3
    m = gates.shape[1]
    halo_per_block = ROW_BLOCK // SUBLANES
    main = (None, ROW_BLOCK, LANES)
    halo = (None, SUBLANES, LANES)

    def halo_idx(offset):
        return lambda cb, i: (offset + cb, jnp.maximum(i * halo_per_block - 1, 0), 0)

    return pl.pallas_call(
        _gated_conv_body,
        grid=(nch, m // ROW_BLOCK),
        in_specs=[
            pl.BlockSpec(main, lambda cb, i: (cb, i, 0)),
            pl.BlockSpec(main, lambda cb, i: (nch + cb, i, 0)),
            pl.BlockSpec(main, lambda cb, i: (2 * nch + cb, i, 0)),
            pl.BlockSpec(halo, halo_idx(nch)),
            pl.BlockSpec(halo, halo_idx(2 * nch)),
            pl.BlockSpec((CONV_K, LANES), lambda cb, i: (0, cb)),
            pl.BlockSpec((1, LANES), lambda cb, i: (0, cb)),
        ],
        out_specs=pl.BlockSpec((ROW_BLOCK, LANES), lambda cb, i: (i, cb)),
        out_shape=jax.ShapeDtypeStruct((m, nch * LANES), BF16),
        scratch_shapes=[pltpu.VMEM((ROW_BLOCK + SUBLANES, LANES), F32)],
        compiler_params=_params(("parallel", "arbitrary")),
        name="gated_conv",
    )(gates, gates, gates, gates, gates, conv_w, conv_b.reshape(1, -1))


def _residual_norm_store(m_ref, h_ref, g_ref, o_ref):
    nchunk, _, tn = m_ref.shape
    ss = None
    for c in range(nchunk):
        mc = m_ref[c]
        part = jnp.sum(mc * mc, axis=-1, keepdims=True)
        ss = part if ss is None else ss + part
    scale = lax.rsqrt(ss / (nchunk * tn) + NORM_EPS)
    for c in range(nchunk):
        cols = slice(c * tn, (c + 1) * tn)
        o_ref[:, cols] = h_ref[:, cols] + m_ref[c] * scale * g_ref[:, cols]


def _out_proj_body(xa_ref, xb_ref, wa_ref, wb_ref, h_ref, g_ref, o_ref, m_ref):
    j = pl.program_id(1)
    m_ref[j] = (jnp.dot(xa_ref[...], wa_ref[...], preferred_element_type=F32)
                + jnp.dot(xb_ref[...], wb_ref[...], preferred_element_type=F32))

    @pl.when(j == pl.num_programs(1) - 1)
    def _():
        _residual_norm_store(m_ref, h_ref, g_ref, o_ref)


def _out_proj_residual(xa, xb, w, h, g, tn=512):
    m, ka = xa.shape
    kb = xb.shape[1]
    assert ka == kb
    n = w.shape[1]
    return pl.pallas_call(
        _out_proj_body,
        grid=(m // ROW_BLOCK, n // tn),
        in_specs=[
            pl.BlockSpec((ROW_BLOCK, ka), lambda i, j: (i, 0)),
            pl.BlockSpec((ROW_BLOCK, kb), lambda i, j: (i, 0)),
            pl.BlockSpec((ka, tn), lambda i, j: (0, j)),
            pl.BlockSpec((kb, tn), lambda i, j: (1, j)),
            pl.BlockSpec((ROW_BLOCK, n), lambda i, j: (i, 0)),
            pl.BlockSpec((1, n), lambda i, j: (0, 0)),
        ],
        out_specs=pl.BlockSpec((ROW_BLOCK, n), lambda i, j: (i, 0)),
        out_shape=jax.ShapeDtypeStruct((m, n), F32),
        scratch_shapes=[pltpu.VMEM((n // tn, ROW_BLOCK, tn), F32)],
        compiler_params=_params(("parallel", "arbitrary")),
        name="out_proj_residual",
    )(xa, xb, w, w, h, g.reshape(1, n))


def _glu_body(x_ref, w1_ref, w2_ref, h_ref, g_ref, o_ref, m_ref):
    j = pl.program_id(1)
    x = x_ref[...]
    a = jnp.dot(x, w1_ref[...], preferred_element_type=F32)
    b = jnp.dot(x, w2_ref[...], preferred_element_type=F32)
    m_ref[j] = a * jax.nn.sigmoid(b)

    @pl.when(j == pl.num_programs(1) - 1)
    def _():
        _residual_norm_store(m_ref, h_ref, g_ref, o_ref)


def _glu_residual(x, w1, w2, h, g, tn=512):
    m, k = x.shape
    n = w1.shape[1]
    return pl.pallas_call(
        _glu_body,
        grid=(m // ROW_BLOCK, n // tn),
        in_specs=[
            pl.BlockSpec((ROW_BLOCK, k), lambda i, j: (i, 0)),
            pl.BlockSpec((k, tn), lambda i, j: (0, j)),
            pl.BlockSpec((k, tn), lambda i, j: (0, j)),
            pl.BlockSpec((ROW_BLOCK, n), lambda i, j: (i, 0)),
            pl.BlockSpec((1, n), lambda i, j: (0, 0)),
        ],
        out_specs=pl.BlockSpec((ROW_BLOCK, n), lambda i, j: (i, 0)),
        out_shape=jax.ShapeDtypeStruct((m, n), F32),
        scratch_shapes=[pltpu.VMEM((n // tn, ROW_BLOCK, tn), F32)],
        compiler_params=_params(("parallel", "arbitrary")),
        name="glu_residual",
    )(x, w1, w2, h, g.reshape(1, n))


def _ffn_body(h_ref, gin_ref, gout_ref, wg_ref, wu_ref, wd_ref, o_ref, hn_ref, acc_ref):
    j = pl.program_id(1)

    @pl.when(j == 0)
    def _():
        x = h_ref[...]
        ms = jnp.mean(x * x, axis=-1, keepdims=True)
        hn_ref[...] = (x * lax.rsqrt(ms + NORM_EPS) * gin_ref[...]).astype(hn_ref.dtype)
        acc_ref[...] = jnp.zeros_like(acc_ref)

    hn = hn_ref[...]
    gate = jnp.dot(hn, wg_ref[...], preferred_element_type=F32)
    up = jnp.dot(hn, wu_ref[...], preferred_element_type=F32)
    act = (gate * jax.nn.sigmoid(gate) * up).astype(BF16)
    acc_ref[...] += jnp.dot(act, wd_ref[...], preferred_element_type=F32)

    @pl.when(j == pl.num_programs(1) - 1)
    def _():
        f = acc_ref[...]
        ms = jnp.mean(f * f, axis=-1, keepdims=True)
        o_ref[...] = h_ref[...] + f * lax.rsqrt(ms + NORM_EPS) * gout_ref[...]


def _ffn_residual(h, g_in, g_out, w_gate, w_up, w_down, th=256):
    m, d = h.shape
    hidden = w_gate.shape[1]
    return pl.pallas_call(
        _ffn_body,
        grid=(m // ROW_BLOCK, hidden // th),
        in_specs=[
            pl.BlockSpec((ROW_BLOCK, d), lambda i, j: (i, 0)),
            pl.BlockSpec((1, d), lambda i, j: (0, 0)),
            pl.BlockSpec((1, d), lambda i, j: (0, 0)),
            pl.BlockSpec((d, th), lambda i, j: (0, j)),
            pl.BlockSpec((d, th), lambda i, j: (0, j)),
            pl.BlockSpec((th, d), lambda i, j: (j, 0)),
        ],
        out_specs=pl.BlockSpec((ROW_BLOCK, d), lambda i, j: (i, 0)),
        out_shape=jax.ShapeDtypeStruct((m, d), F32),
        scratch_shapes=[pltpu.VMEM((ROW_BLOCK, d), BF16),
                        pltpu.VMEM((ROW_BLOCK, d), F32)],
        compiler_params=_params(("parallel", "arbitrary")),
        name="ffn_residual",
    )(h, g_in.reshape(1, d), g_out.reshape(1, d), w_gate, w_up, w_down)


def _s5_discretise_body(are_ref, aim_ref, ls_ref, bre_ref, bim_ref, lbr_ref, lbi_ref, bbr_ref, bbi_ref):
    lam_re = jnp.minimum(are_ref[...], -S5_MIN_DECAY)
    lam_im = aim_ref[...]
    delta = jnp.exp(ls_ref[...])
    mag = jnp.exp(lam_re * delta)
    ang = lam_im * delta
    lb_re = mag * jnp.cos(ang)
    lb_im = mag * jnp.sin(ang)
    den = lam_re * lam_re + lam_im * lam_im
    nr = lb_re - 1.0
    ni = lb_im
    coef_re = (nr * lam_re + ni * lam_im) / den
    coef_im = (ni * lam_re - nr * lam_im) / den
    lbr_ref[...] = lb_re
    lbi_ref[...] = lb_im
    br = bre_ref[...]
    bi = bim_ref[...]
    bbr_ref[...] = coef_re[None] * br - coef_im[None] * bi
    bbi_ref[...] = coef_re[None] * bi + coef_im[None] * br


def _s5_discretise(a_re, a_im, log_step, b_re_t, b_im_t):
    groups, state = a_re.shape
    full2 = pl.BlockSpec((groups, state), lambda: (0, 0))
    full3 = pl.BlockSpec(b_re_t.shape, lambda: (0, 0, 0))
    return pl.pallas_call(
        _s5_discretise_body,
        in_specs=[full2, full2, pl.BlockSpec((groups, 1), lambda: (0, 0)), full3, full3],
        out_specs=[full2, full2, full3, full3],
        out_shape=[jax.ShapeDtypeStruct((groups, state), F32)] * 2
        + [jax.ShapeDtypeStruct(b_re_t.shape, F32)] * 2,
        name="s5_discretise",
    )(a_re, a_im, log_step.reshape(groups, 1), b_re_t, b_im_t)


def _s5_scan_body(u_ref, bmat_ref, cre_ref, cim_ref, lr_ref, li_ref, d_ref, o_ref, slab_ref, carry_ref):
    c = pl.program_id(1)
    nb, ts, _ = u_ref.shape
    half = STATE_TILES

    @pl.when(c == 0)
    def _():
        carry_ref[...] = jnp.zeros_like(carry_ref)

    def slab_base(b, tile):
        return (b * 2 * STATE_TILES + tile) * SLAB_PITCH

    for b in range(nb):
        bu = jnp.dot(u_ref[b].astype(BF16), bmat_ref[...], preferred_element_type=F32)
        for tile in range(2 * STATE_TILES):
            slab_ref[pl.ds(slab_base(b, tile), ts), :] = bu[:, tile * LANES:(tile + 1) * LANES]

    lam_r = lr_ref[...]
    lam_i = li_ref[...]

    def step(t, state):
        new_state = []
        for b in range(nb):
            xr, xi = state[2 * b], state[2 * b + 1]
            re_rows = pl.ds(slab_base(b, 0) + t, STATE_TILES, stride=SLAB_PITCH)
            im_rows = pl.ds(slab_base(b, half) + t, STATE_TILES, stride=SLAB_PITCH)
            nxr = lam_r * xr - lam_i * xi + slab_ref[re_rows, :]
            nxi = lam_r * xi + lam_i * xr + slab_ref[im_rows, :]
            slab_ref[re_rows, :] = nxr
            slab_ref[im_rows, :] = nxi
            new_state += [nxr, nxi]
        return tuple(new_state)

    init = tuple(carry_ref[s] for s in range(2 * nb))
    final = lax.fori_loop(0, ts, step, init, unroll=4)
    for s in range(2 * nb):
        carry_ref[s] = final[s]

    for b in range(nb):
        xr = jnp.concatenate(
            [slab_ref[pl.ds(slab_base(b, tile), ts), :] for tile in range(half)], axis=1)
        xi = jnp.concatenate(
            [slab_ref[pl.ds(slab_base(b, half + tile), ts), :] for tile in range(half)], axis=1)
        y = (jnp.dot(xr.astype(BF16), cre_ref[...], preferred_element_type=F32)
             - jnp.dot(xi.astype(BF16), cim_ref[...], preferred_element_type=F32))
        y = y + d_ref[...] * u_ref[b]
        o_ref[b] = jax.nn.gelu(y).astype(o_ref.dtype)


def _s5_scan(u3, bmat, cre, cim, lam_r, lam_i, d_skip):
    nb, lp, d = u3.shape
    nset = bmat.shape[0]
    return pl.pallas_call(
        _s5_scan_body,
        grid=(nset, lp // SEQ_BLOCK),
        in_specs=[
            pl.BlockSpec((nb, SEQ_BLOCK, S5_SET_IN), lambda s, c: (0, c, s)),
            pl.BlockSpec((None, S5_SET_IN, 2 * S5_SET_STATE), lambda s, c: (s, 0, 0)),
            pl.BlockSpec((None, S5_SET_STATE, S5_SET_IN), lambda s, c: (s, 0, 0)),
            pl.BlockSpec((None, S5_SET_STATE, S5_SET_IN), lambda s, c: (s, 0, 0)),
            pl.BlockSpec((None, STATE_TILES, LANES), lambda s, c: (s, 0, 0)),
            pl.BlockSpec((None, STATE_TILES, LANES), lambda s, c: (s, 0, 0)),
            pl.BlockSpec((1, S5_SET_IN), lambda s, c: (0, s)),
        ],
        out_specs=pl.BlockSpec((nb, SEQ_BLOCK, S5_SET_IN), lambda s, c: (0, c, s)),
        out_shape=jax.ShapeDtypeStruct((nb, lp, d), BF16),
        scratch_shapes=[pltpu.VMEM((nb * 2 * STATE_TILES * SLAB_PITCH, LANES), F32),
                        pltpu.VMEM((2 * nb, STATE_TILES, LANES), F32)],
        compiler_params=_params(("parallel", "arbitrary")),
        name="s5_scan",
    )(u3, bmat, cre, cim, lam_r, lam_i, d_skip.reshape(1, d))


def _block_diag_sets(blocks):
    nset, ng, rows, cols = blocks.shape
    eye = jnp.eye(ng, dtype=bool)[None, :, None, :, None]
    out = jnp.where(eye, blocks[:, :, :, None, :], jnp.zeros((), blocks.dtype))
    return out.reshape(nset, ng * rows, ng * cols)


def _s5_mixer_operands(a_re, a_im, log_step, b_re, b_im, c_re, c_im):
    groups, state = a_re.shape
    nset = groups // S5_SET_GROUPS
    lb_re, lb_im, bb_re_t, bb_im_t = _s5_discretise(
        a_re, a_im, log_step, jnp.transpose(b_re, (2, 0, 1)), jnp.transpose(b_im, (2, 0, 1)))

    def in_blocks(bb_t):
        return jnp.transpose(bb_t.reshape(S5_GROUP, nset, S5_SET_GROUPS, state), (1, 2, 0, 3))

    def out_blocks(cmat):
        return jnp.transpose(cmat.reshape(nset, S5_SET_GROUPS, S5_GROUP, state), (0, 1, 3, 2))

    bmat = jnp.concatenate([_block_diag_sets(in_blocks(bb_re_t)),
                            _block_diag_sets(in_blocks(bb_im_t))], axis=-1).astype(BF16)
    cre = _block_diag_sets(out_blocks(c_re)).astype(BF16)
    cim = _block_diag_sets(out_blocks(c_im)).astype(BF16)
    lam_r = lb_re.reshape(nset, STATE_TILES, LANES)
    lam_i = lb_im.reshape(nset, STATE_TILES, LANES)
    return bmat, cre, cim, lam_r, lam_i


def _attn_conv_layer(h, g, w_in, b_f, conv_w, conv_b, w_o, batch, lp, pad):
    u = _rmsnorm(h, g[0], BF16)
    qkv_w = 3 * ATTN_WIDTH
    w_qkv = w_in[:, :qkv_w].astype(BF16)
    w_fg = jnp.pad(w_in[:, qkv_w:qkv_w + ATTN_HEADS], ((0, 0), (0, LANES - ATTN_HEADS))).astype(BF16)
    w_gates = w_in[:, qkv_w + ATTN_HEADS:].astype(BF16)
    qkv = _matmul_slabs(u, w_qkv, BF16, 768, "qkv_proj")
    gates = _matmul_slabs(u, w_gates, F32, 768, "gate_proj")
    fg = _matmul_slabs(u, w_fg, F32, LANES, "forget_proj")[0]
    b_pad = jnp.pad(b_f, (0, LANES - ATTN_HEADS)).reshape(1, LANES)
    negc = _neg_cum_logf(fg, b_pad, batch, lp, pad)
    negc = jnp.transpose(negc[:, :ATTN_HEADS].reshape(batch, lp, HEAD_PAIRS, 2), (0, 2, 3, 1))
    attn = _fox_attn(qkv, negc, batch, lp)
    conv = _gated_conv(gates, conv_w, conv_b)
    return _out_proj_residual(attn, conv, w_o.astype(BF16), h, g[1])


def _s5_layer(h, g, a_re, a_im, log_step, b_re, b_im, c_re, c_im, d_skip, w_glu1, w_glu2, batch, lp):
    m, d = h.shape
    u = _rmsnorm(h, g[0], F32)
    operands = _s5_mixer_operands(a_re, a_im, log_step, b_re, b_im, c_re, c_im)
    act = _s5_scan(u.reshape(batch, lp, d), *operands, d_skip)
    return _glu_residual(act.reshape(m, d), w_glu1.astype(BF16), w_glu2.astype(BF16), h, g[1])


def kernel(x, meta_tokens, norm_g, ab_w_in, ab_b_f, ab_conv_w, ab_conv_b, ab_w_o, s5_a_re, s5_a_im, s5_log_step, s5_b_re, s5_b_im, s5_c_re, s5_c_im, s5_d, s5_w_glu1, s5_w_glu2, ffn_w_gate, ffn_w_up, ffn_w_down):
    batch, seq, d = x.shape
    depth = norm_g.shape[0]
    tokens = N_META + seq
    lp = pl.cdiv(tokens, SEQ_BLOCK) * SEQ_BLOCK
    assert (batch * lp) % ROW_BLOCK == 0
    pad = lp - tokens
    meta = jnp.broadcast_to(meta_tokens.astype(x.dtype)[None], (batch, N_META, d))
    h = jnp.concatenate([jnp.zeros((batch, pad, d), x.dtype), meta, x], axis=1).reshape(batch * lp, d)
    for i in range(depth):
        g = norm_g[i]
        j = i // 2
        if i % 2 == 0:
            h = _attn_conv_layer(h, g, ab_w_in[j], ab_b_f[j], ab_conv_w[j], ab_conv_b[j], ab_w_o[j],
                                 batch, lp, pad)
        else:
            h = _s5_layer(h, g, s5_a_re[j], s5_a_im[j], s5_log_step[j], s5_b_re[j], s5_b_im[j],
                          s5_c_re[j], s5_c_im[j], s5_d[j], s5_w_glu1[j], s5_w_glu2[j], batch, lp)
        h = _ffn_residual(h, g[2], g[3], ffn_w_gate[i].astype(BF16), ffn_w_up[i].astype(BF16),
                          ffn_w_down[i].astype(BF16))
    return h.reshape(batch, lp, d)[:, pad + N_META:]
```

```python
import functools
import math

import jax
import jax.numpy as jnp
from jax import lax
from jax.experimental import pallas as pl
from jax.experimental.pallas import tpu as pltpu

F32 = jnp.float32
BF16 = jnp.bfloat16

N_META = 16
ATTN_HEADS = 16
HEAD_DIM = 64
ATTN_WIDTH = ATTN_HEADS * HEAD_DIM
CONV_K = 3
S5_GROUP = 16
S5_STATE = 64
S5_MIN_DECAY = 1e-4
NORM_EPS = 1e-6

LANES = 128
SUBLANES = 8
ROW_BLOCK = 768
SEQ_BLOCK = 384
HEAD_PAIRS = ATTN_HEADS * HEAD_DIM // LANES
S5_SET_GROUPS = 16
S5_SET_IN = S5_SET_GROUPS * S5_GROUP
S5_SET_STATE = S5_SET_GROUPS * S5_STATE
STATE_TILES = S5_SET_STATE // LANES
SLAB_PITCH = SEQ_BLOCK + SUBLANES
KEY_MASK = -1e30
VMEM_LIMIT = 60 * 1024 * 1024


def _params(sem):
    return pltpu.CompilerParams(dimension_semantics=sem, vmem_limit_bytes=VMEM_LIMIT)


def _rmsnorm_body(x_ref, g_ref, o_ref):
    x = x_ref[...]
    ms = jnp.mean(x * x, axis=-1, keepdims=True)
    o_ref[...] = (x * lax.rsqrt(ms + NORM_EPS) * g_ref[...]).astype(o_ref.dtype)


def _rmsnorm(h, g, out_dtype):
    m, d = h.shape
    return pl.pallas_call(
        _rmsnorm_body,
        grid=(m // ROW_BLOCK,),
        in_specs=[pl.BlockSpec((ROW_BLOCK, d), lambda i: (i, 0)),
                  pl.BlockSpec((1, d), lambda i: (0, 0))],
        out_specs=pl.BlockSpec((ROW_BLOCK, d), lambda i: (i, 0)),
        out_shape=jax.ShapeDtypeStruct((m, d), out_dtype),
        compiler_params=_params(("parallel",)),
        name="rmsnorm",
    )(h, g.reshape(1, d))


def _matmul_slabs_body(x_ref, w_ref, o_ref, *, nslab):
    r = jnp.dot(x_ref[...], w_ref[...], preferred_element_type=F32)
    for c in range(nslab):
        o_ref[c] = r[:, c * LANES:(c + 1) * LANES].astype(o_ref.dtype)


def _matmul_slabs(x, w, out_dtype, tn, name):
    m, k = x.shape
    n = w.shape[1]
    nslab = tn // LANES
    return pl.pallas_call(
        functools.partial(_matmul_slabs_body, nslab=nslab),
        grid=(m // ROW_BLOCK, n // tn),
        in_specs=[pl.BlockSpec((ROW_BLOCK, k), lambda i, j: (i, 0)),
                  pl.BlockSpec((k, tn), lambda i, j: (0, j))],
        out_specs=pl.BlockSpec((nslab, ROW_BLOCK, LANES), lambda i, j: (j, i, 0)),
        out_shape=jax.ShapeDtypeStruct((n // LANES, m, LANES), out_dtype),
        compiler_params=_params(("parallel", "parallel")),
        name=name,
    )(x, w)


def _neg_cum_logf_body(fg_ref, b_ref, o_ref, carry_ref, *, pad):
    c = pl.program_id(1)

    @pl.when(c == 0)
    def _():
        carry_ref[...] = jnp.zeros_like(carry_ref)

    z = fg_ref[...] + b_ref[...]
    logf = jnp.minimum(z, 0.0) - jnp.log1p(jnp.exp(-jnp.abs(z)))
    t = logf.shape[0]
    row = lax.broadcasted_iota(jnp.int32, (t, t), 0)
    col = lax.broadcasted_iota(jnp.int32, (t, t), 1)
    tri = (col <= row).astype(BF16)
    hi = logf.astype(BF16)
    rem = logf - hi.astype(F32)
    mid = rem.astype(BF16)
    lo = (rem - mid.astype(F32)).astype(BF16)
    cs = (jnp.dot(tri, hi, preferred_element_type=F32)
          + jnp.dot(tri, mid, preferred_element_type=F32)
          + jnp.dot(tri, lo, preferred_element_type=F32)) + carry_ref[...]
    carry_ref[...] = cs[t - 1:t, :]
    pos = c * t + lax.broadcasted_iota(jnp.int32, (t, 1), 0)
    o_ref[...] = jnp.where(pos < pad, KEY_MASK, -cs)


def _neg_cum_logf(fg, b_f, batch, lp, pad):
    m = fg.shape[0]
    nblk = lp // SEQ_BLOCK
    return pl.pallas_call(
        functools.partial(_neg_cum_logf_body, pad=pad),
        grid=(batch, nblk),
        in_specs=[pl.BlockSpec((SEQ_BLOCK, LANES), lambda b, c: (b * nblk + c, 0)),
                  pl.BlockSpec((1, LANES), lambda b, c: (0, 0))],
        out_specs=pl.BlockSpec((SEQ_BLOCK, LANES), lambda b, c: (b * nblk + c, 0)),
        out_shape=jax.ShapeDtypeStruct((m, LANES), F32),
        scratch_shapes=[pltpu.VMEM((1, LANES), F32)],
        compiler_params=_params(("parallel", "arbitrary")),
        name="neg_cum_logf",
    )(fg, b_f)


def _fox_attn_body(q_ref, k_ref, v_ref, nc_ref, o_ref, qh_ref, vaug_ref, s_ref, m_ref, acc_ref):
    qi = pl.program_id(2)
    tq = q_ref.shape[0]
    nblk = vaug_ref.shape[0]
    lane = lax.broadcasted_iota(jnp.int32, (1, LANES), 1)
    first_head = lane < HEAD_DIM

    @pl.when(qi == 0)
    def _():
        ones_a = jnp.broadcast_to((lane == 0).astype(BF16), (tq, LANES))
        ones_b = jnp.broadcast_to((lane == 1).astype(BF16), (tq, LANES))
        for kb in range(nblk):
            v = v_ref[kb * tq:(kb + 1) * tq, :]
            zero = jnp.zeros_like(v)
            vaug_ref[kb, 0:tq, 0:LANES] = jnp.where(first_head, v, zero)
            vaug_ref[kb, tq:2 * tq, 0:LANES] = jnp.where(first_head, zero, v)
            vaug_ref[kb, 0:tq, LANES:2 * LANES] = ones_a
            vaug_ref[kb, tq:2 * tq, LANES:2 * LANES] = ones_b

    q = q_ref[...]
    zero = jnp.zeros_like(q)
    qh_ref[0:tq, :] = jnp.where(first_head, q, zero) * (HEAD_DIM ** -0.5)
    qh_ref[tq:2 * tq, :] = jnp.where(first_head, zero, q) * (HEAD_DIM ** -0.5)
    m_ref[...] = jnp.full_like(m_ref, -jnp.inf)
    acc_ref[...] = jnp.zeros_like(acc_ref)

    def scores(ki, slot):
        rows = pl.ds(pl.multiple_of(ki * tq, tq), tq)
        s = lax.dot_general(qh_ref[...], k_ref[rows, :], (((1,), (1,)), ((), ())),
                            preferred_element_type=F32)
        nc = nc_ref[ki]
        s_ref[slot, 0:tq, :] = s[0:tq] + nc[0:1, :]
        s_ref[slot, tq:2 * tq, :] = s[tq:2 * tq] + nc[1:2, :]

    def accumulate(ki, slot, diagonal):
        s = s_ref[slot]
        if diagonal:
            row = lax.broadcasted_iota(jnp.int32, (2 * tq, tq), 0)
            row = jnp.where(row >= tq, row - tq, row)
            col = lax.broadcasted_iota(jnp.int32, (2 * tq, tq), 1)
            s = jnp.where(col <= row, s, -jnp.inf)
        m_prev = m_ref[...]
        m_next = jnp.maximum(m_prev, jnp.max(s, axis=-1, keepdims=True))
        m_ref[...] = m_next
        alpha = jnp.exp(m_prev - m_next)
        p = jnp.exp(s - jnp.concatenate([m_next] * (tq // LANES), axis=1)).astype(BF16)
        pcat = jnp.concatenate([p[0:tq], p[tq:2 * tq]], axis=1)
        pv = jnp.dot(pcat, vaug_ref[ki], preferred_element_type=F32)
        alpha_a = alpha[0:tq]
        alpha_b = alpha[tq:2 * tq]
        scale = jnp.concatenate([jnp.where(first_head, alpha_a, alpha_b),
                                 jnp.where(lane == 0, alpha_a, alpha_b)], axis=1)
        acc_ref[...] = scale * acc_ref[...] + pv

    scores(0, 0)

    def off_diagonal_pair(j, carry):
        accumulate(2 * j, 0, False)
        scores(2 * j + 1, 1)
        accumulate(2 * j + 1, 1, False)
        scores(2 * j + 2, 0)
        return carry

    lax.fori_loop(0, qi // 2, off_diagonal_pair, 0)
    odd = lax.rem(qi, 2)

    @pl.when(odd == 1)
    def _():
        accumulate(qi - 1, 0, False)
        scores(qi, 1)

    accumulate(qi, odd, True)
    acc = acc_ref[...]
    sum_a = jnp.broadcast_to(acc[:, LANES:LANES + 1], (tq, LANES))
    sum_b = jnp.broadcast_to(acc[:, LANES + 1:LANES + 2], (tq, LANES))
    o_ref[...] = (acc[:, 0:LANES] / jnp.where(first_head, sum_a, sum_b)).astype(o_ref.dtype)


def _fox_attn(qkv, negc, batch, lp):
    nslab, m, _ = qkv.shape
    nq = lp // SEQ_BLOCK
    qkv_seq = qkv.reshape(nslab, batch, lp, LANES)
    seq_blk = (None, None, lp, LANES)
    return pl.pallas_call(
        _fox_attn_body,
        grid=(batch, HEAD_PAIRS, nq),
        in_specs=[
            pl.BlockSpec((None, SEQ_BLOCK, LANES), lambda b, p, qi: (p, b * nq + qi, 0)),
            pl.BlockSpec(seq_blk, lambda b, p, qi: (HEAD_PAIRS + p, b, 0, 0)),
            pl.BlockSpec(seq_blk, lambda b, p, qi: (2 * HEAD_PAIRS + p, b, 0, 0)),
            pl.BlockSpec((None, None, nq, 2, SEQ_BLOCK), lambda b, p, qi: (b, p, 0, 0, 0)),
        ],
        out_specs=pl.BlockSpec((SEQ_BLOCK, LANES), lambda b, p, qi: (b * nq + qi, p)),
        out_shape=jax.ShapeDtypeStruct((m, ATTN_WIDTH), BF16),
        scratch_shapes=[pltpu.VMEM((2 * SEQ_BLOCK, LANES), BF16),
                        pltpu.VMEM((nq, 2 * SEQ_BLOCK, 2 * LANES), BF16),
                        pltpu.VMEM((2, 2 * SEQ_BLOCK, SEQ_BLOCK), F32),
                        pltpu.VMEM((2 * SEQ_BLOCK, LANES), F32),
                        pltpu.VMEM((SEQ_BLOCK, 2 * LANES), F32)],
        compiler_params=_params(("parallel", "parallel", "arbitrary")),
        name="fox_attn",
    )(qkv, qkv_seq, qkv_seq, negc)


def _gated_conv_body(gb_ref, gc_ref, xc_ref, gch_ref, xch_ref, w_ref, b_ref, o_ref, z_ref):
    i = pl.program_id(1)
    tm = gb_ref.shape[0]
    z = gc_ref[...] * xc_ref[...]
    halo = gch_ref[...] * xch_ref[...]
    z_ref[0:SUBLANES, :] = jnp.where(i > 0, halo, jnp.zeros_like(halo))
    z_ref[SUBLANES:SUBLANES + tm, :] = z
    z1 = z_ref[SUBLANES - 1:SUBLANES - 1 + tm, :]
    z2 = z_ref[SUBLANES - 2:SUBLANES - 2 + tm, :]
    conv = w_ref[0:1, :] * z2 + w_ref[1:2, :] * z1 + w_ref[2:3, :] * z + b_ref[...]
    o_ref[...] = (gb_ref[...] * conv).astype(o_ref.dtype)


def _gated_conv(gates, conv_w, conv_b):
    nch = gates.shape[0] // 3
    m = gates.shape[1]
    halo_per_block = ROW_BLOCK // SUBLANES
    main = (None, ROW_BLOCK, LANES)
    halo = (None, SUBLANES, LANES)

    def halo_idx(offset):
        return lambda cb, i: (offset + cb, jnp.maximum(i * halo_per_block - 1, 0), 0)

    return pl.pallas_call(
        _gated_conv_body,
        grid=(nch, m // ROW_BLOCK),
        in_specs=[
            pl.BlockSpec(main, lambda cb, i: (cb, i, 0)),
            pl.BlockSpec(main, lambda cb, i: (nch + cb, i, 0)),
            pl.BlockSpec(main, lambda cb, i: (2 * nch + cb, i, 0)),
            pl.BlockSpec(halo, halo_idx(nch)),
            pl.BlockSpec(halo, halo_idx(2 * nch)),
            pl.BlockSpec((CONV_K, LANES), lambda cb, i: (0, cb)),
            pl.BlockSpec((1, LANES), lambda cb, i: (0, cb)),
        ],
        out_specs=pl.BlockSpec((ROW_BLOCK, LANES), lambda cb, i: (i, cb)),
        out_shape=jax.ShapeDtypeStruct((m, nch * LANES), BF16),
        scratch_shapes=[pltpu.VMEM((ROW_BLOCK + SUBLANES, LANES), F32)],
        compiler_params=_params(("parallel", "arbitrary")),
        name="gated_conv",
    )(gates, gates, gates, gates, gates, conv_w, conv_b.reshape(1, -1))


def _residual_norm_store(m_ref, h_ref, g_ref, o_ref):
    nchunk, _, tn = m_ref.shape
    ss = None
    for c in range(nchunk):
        mc = m_ref[c]
        part = jnp.sum(mc * mc, axis=-1, keepdims=True)
        ss = part if ss is None else ss + part
    scale = lax.rsqrt(ss / (nchunk * tn) + NORM_EPS)
    for c in range(nchunk):
        cols = slice(c * tn, (c + 1) * tn)
        o_ref[:, cols] = h_ref[:, cols] + m_ref[c] * scale * g_ref[:, cols]


def _out_proj_body(xa_ref, xb_ref, wa_ref, wb_ref, h_ref, g_ref, o_ref, m_ref):
    j = pl.program_id(1)
    m_ref[j] = (jnp.dot(xa_ref[...], wa_ref[...], preferred_element_type=F32)
                + jnp.dot(xb_ref[...], wb_ref[...], preferred_element_type=F32))

    @pl.when(j == pl.num_programs(1) - 1)
    def _():
        _residual_norm_store(m_ref, h_ref, g_ref, o_ref)


def _out_proj_residual(xa, xb, w, h, g, tn=512):
    m, ka = xa.shape
    kb = xb.shape[1]
    assert ka == kb
    n = w.shape[1]
    return pl.pallas_call(
        _out_proj_body,
        grid=(m // ROW_BLOCK, n // tn),
        in_specs=[
            pl.BlockSpec((ROW_BLOCK, ka), lambda i, j: (i, 0)),
            pl.BlockSpec((ROW_BLOCK, kb), lambda i, j: (i, 0)),
            pl.BlockSpec((ka, tn), lambda i, j: (0, j)),
            pl.BlockSpec((kb, tn), lambda i, j: (1, j)),
            pl.BlockSpec((ROW_BLOCK, n), lambda i, j: (i, 0)),
            pl.BlockSpec((1, n), lambda i, j: (0, 0)),
        ],
        out_specs=pl.BlockSpec((ROW_BLOCK, n), lambda i, j: (i, 0)),
        out_shape=jax.ShapeDtypeStruct((m, n), F32),
        scratch_shapes=[pltpu.VMEM((n // tn, ROW_BLOCK, tn), F32)],
        compiler_params=_params(("parallel", "arbitrary")),
        name="out_proj_residual",
    )(xa, xb, w, w, h, g.reshape(1, n))


def _glu_body(x_ref, w1_ref, w2_ref, h_ref, g_ref, o_ref, m_ref):
    j = pl.program_id(1)
    x = x_ref[...]
    a = jnp.dot(x, w1_ref[...], preferred_element_type=F32)
    b = jnp.dot(x, w2_ref[...], preferred_element_type=F32)
    m_ref[j] = a * jax.nn.sigmoid(b)

    @pl.when(j == pl.num_programs(1) - 1)
    def _():
        _residual_norm_store(m_ref, h_ref, g_ref, o_ref)


def _glu_residual(x, w1, w2, h, g, tn=512):
    m, k = x.shape
    n = w1.shape[1]
    return pl.pallas_call(
        _glu_body,
        grid=(m // ROW_BLOCK, n // tn),
        in_specs=[
            pl.BlockSpec((ROW_BLOCK, k), lambda i, j: (i, 0)),
            pl.BlockSpec((k, tn), lambda i, j: (0, j)),
            pl.BlockSpec((k, tn), lambda i, j: (0, j)),
            pl.BlockSpec((ROW_BLOCK, n), lambda i, j: (i, 0)),
            pl.BlockSpec((1, n), lambda i, j: (0, 0)),
        ],
        out_specs=pl.BlockSpec((ROW_BLOCK, n), lambda i, j: (i, 0)),
        out_shape=jax.ShapeDtypeStruct((m, n), F32),
        scratch_shapes=[pltpu.VMEM((n // tn, ROW_BLOCK, tn), F32)],
        compiler_params=_params(("parallel", "arbitrary")),
        name="glu_residual",
    )(x, w1, w2, h, g.reshape(1, n))


def _ffn_body(h_ref, gin_ref, gout_ref, wg_ref, wu_ref, wd_ref, o_ref, hn_ref, acc_ref):
    j = pl.program_id(1)

    @pl.when(j == 0)
    def _():
        x = h_ref[...]
        ms = jnp.mean(x * x, axis=-1, keepdims=True)
        hn_ref[...] = (x * lax.rsqrt(ms + NORM_EPS) * gin_ref[...]).astype(hn_ref.dtype)
        acc_ref[...] = jnp.zeros_like(acc_ref)

    hn = hn_ref[...]
    gate = jnp.dot(hn, wg_ref[...], preferred_element_type=F32)
    up = jnp.dot(hn, wu_ref[...], preferred_element_type=F32)
    act = (gate * jax.nn.sigmoid(gate) * up).astype(BF16)
    acc_ref[...] += jnp.dot(act, wd_ref[...], preferred_element_type=F32)

    @pl.when(j == pl.num_programs(1) - 1)
    def _():
        f = acc_ref[...]
        ms = jnp.mean(f * f, axis=-1, keepdims=True)
        o_ref[...] = h_ref[...] + f * lax.rsqrt(ms + NORM_EPS) * gout_ref[...]


def _ffn_residual(h, g_in, g_out, w_gate, w_up, w_down, th=512):
    m, d = h.shape
    hidden = w_gate.shape[1]
    return pl.pallas_call(
        _ffn_body,
        grid=(m // ROW_BLOCK, hidden // th),
        in_specs=[
            pl.BlockSpec((ROW_BLOCK, d), lambda i, j: (i, 0)),
            pl.BlockSpec((1, d), lambda i, j: (0, 0)),
            pl.BlockSpec((1, d), lambda i, j: (0, 0)),
            pl.BlockSpec((d, th), lambda i, j: (0, j)),
            pl.BlockSpec((d, th), lambda i, j: (0, j)),
            pl.BlockSpec((th, d), lambda i, j: (j, 0)),
        ],
        out_specs=pl.BlockSpec((ROW_BLOCK, d), lambda i, j: (i, 0)),
        out_shape=jax.ShapeDtypeStruct((m, d), F32),
        scratch_shapes=[pltpu.VMEM((ROW_BLOCK, d), BF16),
                        pltpu.VMEM((ROW_BLOCK, d), F32)],
        compiler_params=_params(("parallel", "arbitrary")),
        name="ffn_residual",
    )(h, g_in.reshape(1, d), g_out.reshape(1, d), w_gate, w_up, w_down)


def _s5_discretise_body(are_ref, aim_ref, ls_ref, bre_ref, bim_ref, lbr_ref, lbi_ref, bbr_ref, bbi_ref):
    lam_re = jnp.minimum(are_ref[...], -S5_MIN_DECAY)
    lam_im = aim_ref[...]
    delta = jnp.exp(ls_ref[...])
    mag = jnp.exp(lam_re * delta)
    ang = lam_im * delta
    lb_re = mag * jnp.cos(ang)
    lb_im = mag * jnp.sin(ang)
    den = lam_re * lam_re + lam_im * lam_im
    nr = lb_re - 1.0
    ni = lb_im
    coef_re = (nr * lam_re + ni * lam_im) / den
    coef_im = (ni * lam_re - nr * lam_im) / den
    lbr_ref[...] = lb_re
    lbi_ref[...] = lb_im
    br = bre_ref[...]
    bi = bim_ref[...]
    bbr_ref[...] = coef_re[None] * br - coef_im[None] * bi
    bbi_ref[...] = coef_re[None] * bi + coef_im[None] * br


def _s5_discretise(a_re, a_im, log_step, b_re_t, b_im_t):
    groups, state = a_re.shape
    full2 = pl.BlockSpec((groups, state), lambda: (0, 0))
    full3 = pl.BlockSpec(b_re_t.shape, lambda: (0, 0, 0))
    return pl.pallas_call(
        _s5_discretise_body,
        in_specs=[full2, full2, pl.BlockSpec((groups, 1), lambda: (0, 0)), full3, full3],
        out_specs=[full2, full2, full3, full3],
        out_shape=[jax.ShapeDtypeStruct((groups, state), F32)] * 2
        + [jax.ShapeDtypeStruct(b_re_t.shape, F32)] * 2,
        name="s5_discretise",
    )(a_re, a_im, log_step.reshape(groups, 1), b_re_t, b_im_t)


def _s5_scan_body(u_ref, bmat_ref, cre_ref, cim_ref, lr_ref, li_ref, d_ref, o_ref, slab_ref, carry_ref):
    c = pl.program_id(1)
    nb, ts, _ = u_ref.shape
    half = STATE_TILES

    @pl.when(c == 0)
    def _():
        carry_ref[...] = jnp.zeros_like(carry_ref)

    def slab_base(b, tile):
        return (b * 2 * STATE_TILES + tile) * SLAB_PITCH

    for b in range(nb):
        bu = jnp.dot(u_ref[b].astype(BF16), bmat_ref[...], preferred_element_type=F32)
        for tile in range(2 * STATE_TILES):
            slab_ref[pl.ds(slab_base(b, tile), ts), :] = bu[:, tile * LANES:(tile + 1) * LANES]

    lam_r = lr_ref[...]
    lam_i = li_ref[...]

    def step(t, state):
        new_state = []
        for b in range(nb):
            xr, xi = state[2 * b], state[2 * b + 1]
            re_rows = pl.ds(slab_base(b, 0) + t, STATE_TILES, stride=SLAB_PITCH)
            im_rows = pl.ds(slab_base(b, half) + t, STATE_TILES, stride=SLAB_PITCH)
            nxr = lam_r * xr - lam_i * xi + slab_ref[re_rows, :]
            nxi = lam_r * xi + lam_i * xr + slab_ref[im_rows, :]
            slab_ref[re_rows, :] = nxr
            slab_ref[im_rows, :] = nxi
            new_state += [nxr, nxi]
        return tuple(new_state)

    init = tuple(carry_ref[s] for s in range(2 * nb))
    final = lax.fori_loop(0, ts, step, init, unroll=4)
    for s in range(2 * nb):
        carry_ref[s] = final[s]

    for b in range(nb):
        xr = jnp.concatenate(
            [slab_ref[pl.ds(slab_base(b, tile), ts), :] for tile in range(half)], axis=1)
        xi = jnp.concatenate(
            [slab_ref[pl.ds(slab_base(b, half + tile), ts), :] for tile in range(half)], axis=1)
        y = (jnp.dot(xr.astype(BF16), cre_ref[...], preferred_element_type=F32)
             - jnp.dot(xi.astype(BF16), cim_ref[...], preferred_element_type=F32))
        y = y + d_ref[...] * u_ref[b]
        o_ref[b] = jax.nn.gelu(y).astype(o_ref.dtype)


def _s5_scan(u3, bmat, cre, cim, lam_r, lam_i, d_skip):
    nb, lp, d = u3.shape
    nset = bmat.shape[0]
    return pl.pallas_call(
        _s5_scan_body,
        grid=(nset, lp // SEQ_BLOCK),
        in_specs=[
            pl.BlockSpec((nb, SEQ_BLOCK, S5_SET_IN), lambda s, c: (0, c, s)),
            pl.BlockSpec((None, S5_SET_IN, 2 * S5_SET_STATE), lambda s, c: (s, 0, 0)),
            pl.BlockSpec((None, S5_SET_STATE, S5_SET_IN), lambda s, c: (s, 0, 0)),
            pl.BlockSpec((None, S5_SET_STATE, S5_SET_IN), lambda s, c: (s, 0, 0)),
            pl.BlockSpec((None, STATE_TILES, LANES), lambda s, c: (s, 0, 0)),
            pl.BlockSpec((None, STATE_TILES, LANES), lambda s, c: (s, 0, 0)),
            pl.BlockSpec((1, S5_SET_IN), lambda s, c: (0, s)),
        ],
        out_specs=pl.BlockSpec((nb, SEQ_BLOCK, S5_SET_IN), lambda s, c: (0, c, s)),
        out_shape=jax.ShapeDtypeStruct((nb, lp, d), BF16),
        scratch_shapes=[pltpu.VMEM((nb * 2 * STATE_TILES * SLAB_PITCH, LANES), F32),
                        pltpu.VMEM((2 * nb, STATE_TILES, LANES), F32)],
        compiler_params=_params(("parallel", "arbitrary")),
        name="s5_scan",
    )(u3, bmat, cre, cim, lam_r, lam_i, d_skip.reshape(1, d))


def _block_diag_sets(blocks):
    nset, ng, rows, cols = blocks.shape
    eye = jnp.eye(ng, dtype=bool)[None, :, None, :, None]
    out = jnp.where(eye, blocks[:, :, :, None, :], jnp.zeros((), blocks.dtype))
    return out.reshape(nset, ng * rows, ng * cols)


def _s5_mixer_operands(a_re, a_im, log_step, b_re, b_im, c_re, c_im):
    groups, state = a_re.shape
    nset = groups // S5_SET_GROUPS
    lb_re, lb_im, bb_re_t, bb_im_t = _s5_discretise(
        a_re, a_im, log_step, jnp.transpose(b_re, (2, 0, 1)), jnp.transpose(b_im, (2, 0, 1)))

    def in_blocks(bb_t):
        return jnp.transpose(bb_t.reshape(S5_GROUP, nset, S5_SET_GROUPS, state), (1, 2, 0, 3))

    def out_blocks(cmat):
        return jnp.transpose(cmat.reshape(nset, S5_SET_GROUPS, S5_GROUP, state), (0, 1, 3, 2))

    bmat = jnp.concatenate([_block_diag_sets(in_blocks(bb_re_t)),
                            _block_diag_sets(in_blocks(bb_im_t))], axis=-1).astype(BF16)
    cre = _block_diag_sets(out_blocks(c_re)).astype(BF16)
    cim = _block_diag_sets(out_blocks(c_im)).astype(BF16)
    lam_r = lb_re.reshape(nset, STATE_TILES, LANES)
    lam_i = lb_im.reshape(nset, STATE_TILES, LANES)
    return bmat, cre, cim, lam_r, lam_i


def _attn_conv_layer(h, g, w_in, b_f, conv_w, conv_b, w_o, batch, lp, pad):
    u = _rmsnorm(h, g[0], BF16)
    qkv_w = 3 * ATTN_WIDTH
    w_qkv = w_in[:, :qkv_w].astype(BF16)
    w_fg = jnp.pad(w_in[:, qkv_w:qkv_w + ATTN_HEADS], ((0, 0), (0, LANES - ATTN_HEADS))).astype(BF16)
    w_gates = w_in[:, qkv_w + ATTN_HEADS:].astype(BF16)
    qkv = _matmul_slabs(u, w_qkv, BF16, 768, "qkv_proj")
    gates = _matmul_slabs(u, w_gates, F32, 768, "gate_proj")
    fg = _matmul_slabs(u, w_fg, F32, LANES, "forget_proj")[0]
    b_pad = jnp.pad(b_f, (0, LANES - ATTN_HEADS)).reshape(1, LANES)
    negc = _neg_cum_logf(fg, b_pad, batch, lp, pad)
    negc = jnp.transpose(negc[:, :ATTN_HEADS].reshape(batch, lp // SEQ_BLOCK, SEQ_BLOCK, HEAD_PAIRS, 2),
                         (0, 3, 1, 4, 2))
    attn = _fox_attn(qkv, negc, batch, lp)
    conv = _gated_conv(gates, conv_w, conv_b)
    return _out_proj_residual(attn, conv, w_o.astype(BF16), h, g[1])


def _s5_layer(h, g, a_re, a_im, log_step, b_re, b_im, c_re, c_im, d_skip, w_glu1, w_glu2, batch, lp):
    m, d = h.shape
    u = _rmsnorm(h, g[0], F32)
    operands = _s5_mixer_operands(a_re, a_im, log_step, b_re, b_im, c_re, c_im)
    act = _s5_scan(u.reshape(batch, lp, d), *operands, d_skip)
    return _glu_residual(act.reshape(m, d), w_glu1.astype(BF16), w_glu2.astype(BF16), h, g[1])


def kernel(x, meta_tokens, norm_g, ab_w_in, ab_b_f, ab_conv_w, ab_conv_b, ab_w_o, s5_a_re, s5_a_im, s5_log_step, s5_b_re, s5_b_im, s5_c_re, s5_c_im, s5_d, s5_w_glu1, s5_w_glu2, ffn_w_gate, ffn_w_up, ffn_w_down):
    batch, seq, d = x.shape
    depth = norm_g.shape[0]
    tokens = N_META + seq
    lp = pl.cdiv(tokens, SEQ_BLOCK) * SEQ_BLOCK
    assert (batch * lp) % ROW_BLOCK == 0
    pad = lp - tokens
    meta = jnp.broadcast_to(meta_tokens.astype(x.dtype)[None], (batch, N_META, d))
    h = jnp.concatenate([jnp.zeros((batch, pad, d), x.dtype), meta, x], axis=1).reshape(batch * lp, d)
    for i in range(depth):
        g = norm_g[i]
        j = i // 2
        if i % 2 == 0:
            h = _attn_conv_layer(h, g, ab_w_in[j], ab_b_f[j], ab_conv_w[j], ab_conv_b[j], ab_w_o[j],
                                 batch, lp, pad)
        else:
            h = _s5_layer(h, g, s5_a_re[j], s5_a_im[j], s5_log_step[j], s5_b_re[j], s5_b_im[j],
                          s5_c_re[j], s5_c_im[j], s5_d[j], s5_w_glu1[j], s5_w_glu2[j], batch, lp)
        h = _ffn_residual(h, g[2], g[3], ffn_w_gate[i].astype(BF16), ffn_w_up[i].astype(BF16),
                          ffn_w_down[i].astype(BF16))
    return h.reshape(batch, lp, d)[:, pad + N_META:]
```

```python
import functools
import math

import jax
import jax.numpy as jnp
from jax import lax
from jax.experimental import pallas as pl
from jax.experimental.pallas import tpu as pltpu

F32 = jnp.float32
BF16 = jnp.bfloat16

N_META = 16
ATTN_HEADS = 16
HEAD_DIM = 64
ATTN_WIDTH = ATTN_HEADS * HEAD_DIM
CONV_K = 3
S5_GROUP = 16
S5_STATE = 64
S5_MIN_DECAY = 1e-4
NORM_EPS = 1e-6

LANES = 128
SUBLANES = 8
ROW_BLOCK = 768
SEQ_BLOCK = 384
HEAD_PAIRS = ATTN_HEADS * HEAD_DIM // LANES
S5_SET_GROUPS = 16
S5_SET_IN = S5_SET_GROUPS * S5_GROUP
S5_SET_STATE = S5_SET_GROUPS * S5_STATE
S5_SETS_PER_STEP = 2
STATE_TILES = S5_SET_STATE // LANES
SLAB_PITCH = SEQ_BLOCK + SUBLANES
KEY_MASK = -1e30
VMEM_LIMIT = 60 * 1024 * 1024


def _params(sem):
    return pltpu.CompilerParams(dimension_semantics=sem, vmem_limit_bytes=VMEM_LIMIT)


def _rmsnorm_body(x_ref, g_ref, o_ref):
    x = x_ref[...]
    ms = jnp.mean(x * x, axis=-1, keepdims=True)
    o_ref[...] = (x * lax.rsqrt(ms + NORM_EPS) * g_ref[...]).astype(o_ref.dtype)


def _rmsnorm(h, g, out_dtype):
    m, d = h.shape
    return pl.pallas_call(
        _rmsnorm_body,
        grid=(m // ROW_BLOCK,),
        in_specs=[pl.BlockSpec((ROW_BLOCK, d), lambda i: (i, 0)),
                  pl.BlockSpec((1, d), lambda i: (0, 0))],
        out_specs=pl.BlockSpec((ROW_BLOCK, d), lambda i: (i, 0)),
        out_shape=jax.ShapeDtypeStruct((m, d), out_dtype),
        compiler_params=_params(("parallel",)),
        name="rmsnorm",
    )(h, g.reshape(1, d))


def _matmul_slabs_body(x_ref, w_ref, o_ref, *, nslab):
    r = jnp.dot(x_ref[...], w_ref[...], preferred_element_type=F32)
    for c in range(nslab):
        o_ref[c] = r[:, c * LANES:(c + 1) * LANES].astype(o_ref.dtype)


def _matmul_slabs(x, w, layer, col0, n, out_dtype, tn, name):
    m, k = x.shape
    nslab = tn // LANES
    assert col0 % tn == 0 and n % tn == 0
    first = col0 // tn
    return pl.pallas_call(
        functools.partial(_matmul_slabs_body, nslab=nslab),
        grid=(m // ROW_BLOCK, n // tn),
        in_specs=[pl.BlockSpec((ROW_BLOCK, k), lambda i, j: (i, 0)),
                  pl.BlockSpec((None, k, tn), lambda i, j: (layer, 0, first + j))],
        out_specs=pl.BlockSpec((nslab, ROW_BLOCK, LANES), lambda i, j: (j, i, 0)),
        out_shape=jax.ShapeDtypeStruct((n // LANES, m, LANES), out_dtype),
        compiler_params=_params(("parallel", "parallel")),
        name=name,
    )(x, w)


def _neg_cum_logf_body(fg_ref, b_ref, o_ref, carry_ref, *, pad):
    c = pl.program_id(1)

    @pl.when(c == 0)
    def _():
        carry_ref[...] = jnp.zeros_like(carry_ref)

    z = fg_ref[...] + b_ref[...]
    logf = jnp.minimum(z, 0.0) - jnp.log1p(jnp.exp(-jnp.abs(z)))
    t = logf.shape[0]
    row = lax.broadcasted_iota(jnp.int32, (t, t), 0)
    col = lax.broadcasted_iota(jnp.int32, (t, t), 1)
    tri = (col <= row).astype(BF16)
    hi = logf.astype(BF16)
    rem = logf - hi.astype(F32)
    mid = rem.astype(BF16)
    lo = (rem - mid.astype(F32)).astype(BF16)
    cs = (jnp.dot(tri, hi, preferred_element_type=F32)
          + jnp.dot(tri, mid, preferred_element_type=F32)
          + jnp.dot(tri, lo, preferred_element_type=F32)) + carry_ref[...]
    carry_ref[...] = cs[t - 1:t, :]
    pos = c * t + lax.broadcasted_iota(jnp.int32, (t, 1), 0)
    o_ref[...] = jnp.where(pos < pad, KEY_MASK, -cs)


def _neg_cum_logf(fg, b_f, batch, lp, pad):
    m = fg.shape[0]
    nblk = lp // SEQ_BLOCK
    return pl.pallas_call(
        functools.partial(_neg_cum_logf_body, pad=pad),
        grid=(batch, nblk),
        in_specs=[pl.BlockSpec((SEQ_BLOCK, LANES), lambda b, c: (b * nblk + c, 0)),
                  pl.BlockSpec((1, LANES), lambda b, c: (0, 0))],
        out_specs=pl.BlockSpec((SEQ_BLOCK, LANES), lambda b, c: (b * nblk + c, 0)),
        out_shape=jax.ShapeDtypeStruct((m, LANES), F32),
        scratch_shapes=[pltpu.VMEM((1, LANES), F32)],
        compiler_params=_params(("parallel", "arbitrary")),
        name="neg_cum_logf",
    )(fg, b_f)


def _fox_attn_body(q_ref, k_ref, v_ref, nc_ref, o_ref, qh_ref, vaug_ref, s_ref, p_ref, scale_ref, m_ref, acc_ref):
    qi = pl.program_id(2)
    tq = q_ref.shape[0]
    nblk = vaug_ref.shape[0]
    lane = lax.broadcasted_iota(jnp.int32, (1, LANES), 1)
    first_head = lane < HEAD_DIM

    @pl.when(qi == 0)
    def _():
        ones_a = jnp.broadcast_to((lane == 0).astype(BF16), (tq, LANES))
        ones_b = jnp.broadcast_to((lane == 1).astype(BF16), (tq, LANES))
        for kb in range(nblk):
            v = v_ref[kb * tq:(kb + 1) * tq, :]
            zero = jnp.zeros_like(v)
            vaug_ref[kb, 0:tq, 0:LANES] = jnp.where(first_head, v, zero)
            vaug_ref[kb, tq:2 * tq, 0:LANES] = jnp.where(first_head, zero, v)
            vaug_ref[kb, 0:tq, LANES:2 * LANES] = ones_a
            vaug_ref[kb, tq:2 * tq, LANES:2 * LANES] = ones_b

    q = q_ref[...]
    zero = jnp.zeros_like(q)
    qh_ref[0:tq, :] = jnp.where(first_head, q, zero) * (HEAD_DIM ** -0.5)
    qh_ref[tq:2 * tq, :] = jnp.where(first_head, zero, q) * (HEAD_DIM ** -0.5)
    m_ref[...] = jnp.full_like(m_ref, -jnp.inf)
    acc_ref[...] = jnp.zeros_like(acc_ref)

    def scores(ki, slot):
        rows = pl.ds(pl.multiple_of(ki * tq, tq), tq)
        s = lax.dot_general(qh_ref[...], k_ref[rows, :], (((1,), (1,)), ((), ())),
                            preferred_element_type=F32)
        nc = nc_ref[ki]
        s_ref[slot, 0:tq, :] = s[0:tq] + nc[0:1, :]
        s_ref[slot, tq:2 * tq, :] = s[tq:2 * tq] + nc[1:2, :]

    def softmax(slot, diagonal):
        s = s_ref[slot]
        if diagonal:
            row = lax.broadcasted_iota(jnp.int32, (2 * tq, tq), 0)
            row = jnp.where(row >= tq, row - tq, row)
            col = lax.broadcasted_iota(jnp.int32, (2 * tq, tq), 1)
            s = jnp.where(col <= row, s, -jnp.inf)
        m_prev = m_ref[...]
        m_next = jnp.maximum(m_prev, jnp.max(s, axis=-1, keepdims=True))
        m_ref[...] = m_next
        alpha = jnp.exp(m_prev - m_next)
        p = jnp.exp(s - jnp.concatenate([m_next] * (tq // LANES), axis=1)).astype(BF16)
        p_ref[slot, :, 0:tq] = p[0:tq]
        p_ref[slot, :, tq:2 * tq] = p[tq:2 * tq]
        alpha_a = alpha[0:tq]
        alpha_b = alpha[tq:2 * tq]
        scale_ref[slot, :, 0:LANES] = jnp.where(first_head, alpha_a, alpha_b)
        scale_ref[slot, :, LANES:2 * LANES] = jnp.where(lane == 0, alpha_a, alpha_b)

    def accumulate(ki, slot):
        pv = jnp.dot(p_ref[slot], vaug_ref[ki], preferred_element_type=F32)
        acc_ref[...] = scale_ref[slot] * acc_ref[...] + pv

    scores(qi, 0)
    softmax(0, True)
    scores(0, 1)

    def visible_pair(j, carry):
        accumulate(jnp.where(j == 0, qi, 2 * j - 1), 0)
        softmax(1, False)
        scores(2 * j + 1, 0)
        accumulate(2 * j, 1)
        softmax(0, False)
        scores(2 * j + 2, 1)
        return carry

    npairs = qi // 2
    lax.fori_loop(0, npairs, visible_pair, 0)
    accumulate(jnp.where(npairs == 0, qi, 2 * npairs - 1), 0)

    @pl.when(lax.rem(qi, 2) == 1)
    def _():
        softmax(1, False)
        accumulate(qi - 1, 1)

    acc = acc_ref[...]
    sum_a = jnp.broadcast_to(acc[:, LANES:LANES + 1], (tq, LANES))
    sum_b = jnp.broadcast_to(acc[:, LANES + 1:LANES + 2], (tq, LANES))
    o_ref[...] = (acc[:, 0:LANES] / jnp.where(first_head, sum_a, sum_b)).astype(o_ref.dtype)


def _fox_attn(qkv, negc, batch, lp):
    nslab, m, _ = qkv.shape
    nq = lp // SEQ_BLOCK
    qkv_seq = qkv.reshape(nslab, batch, lp, LANES)
    seq_blk = (None, None, lp, LANES)
    return pl.pallas_call(
        _fox_attn_body,
        grid=(batch, HEAD_PAIRS, nq),
        in_specs=[
            pl.BlockSpec((None, SEQ_BLOCK, LANES), lambda b, p, qi: (p, b * nq + qi, 0)),
            pl.BlockSpec(seq_blk, lambda b, p, qi: (HEAD_PAIRS + p, b, 0, 0)),
            pl.BlockSpec(seq_blk, lambda b, p, qi: (2 * HEAD_PAIRS + p, b, 0, 0)),
            pl.BlockSpec((None, None, nq, 2, SEQ_BLOCK), lambda b, p, qi: (b, p, 0, 0, 0)),
        ],
        out_specs=pl.BlockSpec((SEQ_BLOCK, LANES), lambda b, p, qi: (b * nq + qi, p)),
        out_shape=jax.ShapeDtypeStruct((m, ATTN_WIDTH), BF16),
        scratch_shapes=[pltpu.VMEM((2 * SEQ_BLOCK, LANES), BF16),
                        pltpu.VMEM((nq, 2 * SEQ_BLOCK, 2 * LANES), BF16),
                        pltpu.VMEM((2, 2 * SEQ_BLOCK, SEQ_BLOCK), F32),
                        pltpu.VMEM((2, SEQ_BLOCK, 2 * SEQ_BLOCK), BF16),
                        pltpu.VMEM((2, SEQ_BLOCK, 2 * LANES), F32),
                        pltpu.VMEM((2 * SEQ_BLOCK, LANES), F32),
                        pltpu.VMEM((SEQ_BLOCK, 2 * LANES), F32)],
        compiler_params=_params(("parallel", "parallel", "arbitrary")),
        name="fox_attn",
    )(qkv, qkv_seq, qkv_seq, negc)


def _gated_conv_body(gb_ref, gc_ref, xc_ref, gch_ref, xch_ref, w_ref, b_ref, o_ref, z_ref):
    i = pl.program_id(1)
    tm = gb_ref.shape[0]
    z = gc_ref[...] * xc_ref[...]
    halo = gch_ref[...] * xch_ref[...]
    z_ref[0:SUBLANES, :] = jnp.where(i > 0, halo, jnp.zeros_like(halo))
    z_ref[SUBLANES:SUBLANES + tm, :] = z
    z1 = z_ref[SUBLANES - 1:SUBLANES - 1 + tm, :]
    z2 = z_ref[SUBLANES - 2:SUBLANES - 2 + tm, :]
    conv = w_ref[0:1, :] * z2 + w_ref[1:2, :] * z1 + w_ref[2:3, :] * z + b_ref[...]
    o_ref[...] = (gb_ref[...] * conv).astype(o_ref.dtype)


def _gated_conv(gates, conv_w, conv_b):
    nch = gates.shape[0] // 3
    m = gates.shape[1]
    halo_per_block = ROW_BLOCK // SUBLANES
    main = (None, ROW_BLOCK, LANES)
    halo = (None, SUBLANES, LANES)

    def halo_idx(offset):
        return lambda cb, i: (offset + cb, jnp.maximum(i * halo_per_block - 1, 0), 0)

    return pl.pallas_call(
        _gated_conv_body,
        grid=(nch, m // ROW_BLOCK),
        in_specs=[
            pl.BlockSpec(main, lambda cb, i: (cb, i, 0)),
            pl.BlockSpec(main, lambda cb, i: (nch + cb, i, 0)),
            pl.BlockSpec(main, lambda cb, i: (2 * nch + cb, i, 0)),
            pl.BlockSpec(halo, halo_idx(nch)),
            pl.BlockSpec(halo, halo_idx(2 * nch)),
            pl.BlockSpec((CONV_K, LANES), lambda cb, i: (0, cb)),
            pl.BlockSpec((1, LANES), lambda cb, i: (0, cb)),
        ],
        out_specs=pl.BlockSpec((ROW_BLOCK, LANES), lambda cb, i: (i, cb)),
        out_shape=jax.ShapeDtypeStruct((m, nch * LANES), BF16),
        scratch_shapes=[pltpu.VMEM((ROW_BLOCK + SUBLANES, LANES), F32)],
        compiler_params=_params(("parallel", "arbitrary")),
        name="gated_conv",
    )(gates, gates, gates, gates, gates, conv_w, conv_b.reshape(1, -1))


def _residual_norm_store(m_ref, h_ref, g_ref, o_ref):
    nchunk, _, tn = m_ref.shape
    ss = None
    for c in range(nchunk):
        mc = m_ref[c]
        part = jnp.sum(mc * mc, axis=-1, keepdims=True)
        ss = part if ss is None else ss + part
    scale = lax.rsqrt(ss / (nchunk * tn) + NORM_EPS)
    for c in range(nchunk):
        cols = slice(c * tn, (c + 1) * tn)
        o_ref[:, cols] = h_ref[:, cols] + m_ref[c] * scale * g_ref[:, cols]


def _out_proj_body(xa_ref, xb_ref, wa_ref, wb_ref, h_ref, g_ref, o_ref, m_ref):
    j = pl.program_id(1)
    m_ref[j] = (jnp.dot(xa_ref[...], wa_ref[...], preferred_element_type=F32)
                + jnp.dot(xb_ref[...], wb_ref[...], preferred_element_type=F32))

    @pl.when(j == pl.num_programs(1) - 1)
    def _():
        _residual_norm_store(m_ref, h_ref, g_ref, o_ref)


def _out_proj_residual(xa, xb, w, layer, h, g, tn=512):
    m, ka = xa.shape
    kb = xb.shape[1]
    assert ka == kb
    n = w.shape[2]
    return pl.pallas_call(
        _out_proj_body,
        grid=(m // ROW_BLOCK, n // tn),
        in_specs=[
            pl.BlockSpec((ROW_BLOCK, ka), lambda i, j: (i, 0)),
            pl.BlockSpec((ROW_BLOCK, kb), lambda i, j: (i, 0)),
            pl.BlockSpec((None, ka, tn), lambda i, j: (layer, 0, j)),
            pl.BlockSpec((None, kb, tn), lambda i, j: (layer, 1, j)),
            pl.BlockSpec((ROW_BLOCK, n), lambda i, j: (i, 0)),
            pl.BlockSpec((1, n), lambda i, j: (0, 0)),
        ],
        out_specs=pl.BlockSpec((ROW_BLOCK, n), lambda i, j: (i, 0)),
        out_shape=jax.ShapeDtypeStruct((m, n), F32),
        scratch_shapes=[pltpu.VMEM((n // tn, ROW_BLOCK, tn), F32)],
        compiler_params=_params(("parallel", "arbitrary")),
        name="out_proj_residual",
    )(xa, xb, w, w, h, g.reshape(1, n))


def _glu_body(x_ref, w1_ref, w2_ref, h_ref, g_ref, o_ref, m_ref):
    j = pl.program_id(1)
    x = x_ref[...]
    a = jnp.dot(x, w1_ref[...], preferred_element_type=F32)
    b = jnp.dot(x, w2_ref[...], preferred_element_type=F32)
    m_ref[j] = a * jax.nn.sigmoid(b)

    @pl.when(j == pl.num_programs(1) - 1)
    def _():
        _residual_norm_store(m_ref, h_ref, g_ref, o_ref)


def _glu_residual(x, w1, w2, layer, h, g, tn=512):
    m, k = x.shape
    n = w1.shape[2]
    return pl.pallas_call(
        _glu_body,
        grid=(m // ROW_BLOCK, n // tn),
        in_specs=[
            pl.BlockSpec((ROW_BLOCK, k), lambda i, j: (i, 0)),
            pl.BlockSpec((None, k, tn), lambda i, j: (layer, 0, j)),
            pl.BlockSpec((None, k, tn), lambda i, j: (layer, 0, j)),
            pl.BlockSpec((ROW_BLOCK, n), lambda i, j: (i, 0)),
            pl.BlockSpec((1, n), lambda i, j: (0, 0)),
        ],
        out_specs=pl.BlockSpec((ROW_BLOCK, n), lambda i, j: (i, 0)),
        out_shape=jax.ShapeDtypeStruct((m, n), F32),
        scratch_shapes=[pltpu.VMEM((n // tn, ROW_BLOCK, tn), F32)],
        compiler_params=_params(("parallel", "arbitrary")),
        name="glu_residual",
    )(x, w1, w2, h, g.reshape(1, n))


def _ffn_body(h_ref, gin_ref, gout_ref, wg_ref, wu_ref, wd_ref, o_ref, hn_ref, acc_ref):
    j = pl.program_id(1)

    @pl.when(j == 0)
    def _():
        x = h_ref[...]
        ms = jnp.mean(x * x, axis=-1, keepdims=True)
        hn_ref[...] = (x * lax.rsqrt(ms + NORM_EPS) * gin_ref[...]).astype(hn_ref.dtype)
        acc_ref[...] = jnp.zeros_like(acc_ref)

    hn = hn_ref[...]
    gate = jnp.dot(hn, wg_ref[...], preferred_element_type=F32)
    up = jnp.dot(hn, wu_ref[...], preferred_element_type=F32)
    act = (gate * jax.nn.sigmoid(gate) * up).astype(BF16)
    acc_ref[...] += jnp.dot(act, wd_ref[...], preferred_element_type=F32)

    @pl.when(j == pl.num_programs(1) - 1)
    def _():
        f = acc_ref[...]
        ms = jnp.mean(f * f, axis=-1, keepdims=True)
        o_ref[...] = h_ref[...] + f * lax.rsqrt(ms + NORM_EPS) * gout_ref[...]


def _ffn_residual(h, g_in, g_out, w_gate, w_up, w_down, layer, th=512):
    m, d = h.shape
    hidden = w_gate.shape[2]
    return pl.pallas_call(
        _ffn_body,
        grid=(m // ROW_BLOCK, hidden // th),
        in_specs=[
            pl.BlockSpec((ROW_BLOCK, d), lambda i, j: (i, 0)),
            pl.BlockSpec((1, d), lambda i, j: (0, 0)),
            pl.BlockSpec((1, d), lambda i, j: (0, 0)),
            pl.BlockSpec((None, d, th), lambda i, j: (layer, 0, j)),
            pl.BlockSpec((None, d, th), lambda i, j: (layer, 0, j)),
            pl.BlockSpec((None, th, d), lambda i, j: (layer, j, 0)),
        ],
        out_specs=pl.BlockSpec((ROW_BLOCK, d), lambda i, j: (i, 0)),
        out_shape=jax.ShapeDtypeStruct((m, d), F32),
        scratch_shapes=[pltpu.VMEM((ROW_BLOCK, d), BF16),
                        pltpu.VMEM((ROW_BLOCK, d), F32)],
        compiler_params=_params(("parallel", "arbitrary")),
        name="ffn_residual",
    )(h, g_in.reshape(1, d), g_out.reshape(1, d), w_gate, w_up, w_down)


def _lane_repeat(x, reps, exact_f32):
    n = x.shape[1]
    row = lax.broadcasted_iota(jnp.int32, (n, n * reps), 0)
    col = lax.broadcasted_iota(jnp.int32, (n, n * reps), 1)
    sel = (jnp.bitwise_and(col, n - 1) == row).astype(BF16)
    hi = x.astype(BF16)
    out = jnp.dot(hi, sel, preferred_element_type=F32)
    if exact_f32:
        rem = x - hi.astype(F32)
        mid = rem.astype(BF16)
        lo = (rem - mid.astype(F32)).astype(BF16)
        out = out + jnp.dot(mid, sel, preferred_element_type=F32) + jnp.dot(lo, sel, preferred_element_type=F32)
    return out


def _diag_block_mask(shape, row_shift, col_shift):
    row = lax.broadcasted_iota(jnp.int32, shape, 0)
    col = lax.broadcasted_iota(jnp.int32, shape, 1)
    return jnp.right_shift(row, row_shift) == jnp.right_shift(col, col_shift)


def _s5_operands_body(are_ref, aim_ref, ls_ref, bre_ref, bim_ref, cre_ref, cim_ref,
                      lbr_ref, lbi_ref, bmat_ref, cmr_ref, cmi_ref):
    lam_re = jnp.minimum(are_ref[...], -S5_MIN_DECAY)
    lam_im = aim_ref[...]
    delta = jnp.exp(ls_ref[...])
    mag = jnp.exp(lam_re * delta)
    ang = lam_im * delta
    lb_re = mag * jnp.cos(ang)
    lb_im = mag * jnp.sin(ang)
    den = lam_re * lam_re + lam_im * lam_im
    nr = lb_re - 1.0
    ni = lb_im
    coef_re = (nr * lam_re + ni * lam_im) / den
    coef_im = (ni * lam_re - nr * lam_im) / den
    lbr_ref[...] = lb_re
    lbi_ref[...] = lb_im

    br = _lane_repeat(bre_ref[...], S5_SET_GROUPS, True)
    bi = _lane_repeat(bim_ref[...], S5_SET_GROUPS, True)
    in_mask = _diag_block_mask(br.shape, S5_GROUP.bit_length() - 1, S5_STATE.bit_length() - 1)
    zero = jnp.zeros_like(br)
    bmat_ref[:, 0:S5_SET_STATE] = jnp.where(in_mask, coef_re * br - coef_im * bi, zero).astype(bmat_ref.dtype)
    bmat_ref[:, S5_SET_STATE:2 * S5_SET_STATE] = jnp.where(
        in_mask, coef_re * bi + coef_im * br, zero).astype(bmat_ref.dtype)

    cr = _lane_repeat(cre_ref[...], S5_SET_GROUPS, False)
    ci = _lane_repeat(cim_ref[...], S5_SET_GROUPS, False)
    out_mask = _diag_block_mask(cr.shape, S5_STATE.bit_length() - 1, S5_GROUP.bit_length() - 1)
    cmr_ref[...] = jnp.where(out_mask, cr, jnp.zeros_like(cr)).astype(cmr_ref.dtype)
    cmi_ref[...] = jnp.where(out_mask, ci, jnp.zeros_like(ci)).astype(cmi_ref.dtype)


def _s5_operands(a_re, a_im, log_step, b_re, b_im, c_re, c_im):
    groups, state = a_re.shape
    nset = groups // S5_SET_GROUPS
    lane_row = lambda a: a.reshape(nset, 1, S5_SET_STATE)
    step_row = lane_row(jnp.broadcast_to(log_step[:, None], (groups, state)))
    b_rows = lambda b: jnp.transpose(b, (0, 2, 1)).reshape(nset, S5_SET_IN, state)
    c_rows = lambda c: jnp.transpose(c, (0, 2, 1)).reshape(nset, S5_SET_STATE, S5_GROUP)
    row_spec = pl.BlockSpec((None, 1, S5_SET_STATE), lambda s: (s, 0, 0))
    b_spec = pl.BlockSpec((None, S5_SET_IN, state), lambda s: (s, 0, 0))
    c_spec = pl.BlockSpec((None, S5_SET_STATE, S5_GROUP), lambda s: (s, 0, 0))
    cm_spec = pl.BlockSpec((None, S5_SET_STATE, S5_SET_IN), lambda s: (s, 0, 0))
    lb_re, lb_im, bmat, cmr, cmi = pl.pallas_call(
        _s5_operands_body,
        grid=(nset,),
        in_specs=[row_spec, row_spec, row_spec, b_spec, b_spec, c_spec, c_spec],
        out_specs=[row_spec, row_spec,
                   pl.BlockSpec((None, S5_SET_IN, 2 * S5_SET_STATE), lambda s: (s, 0, 0)), cm_spec, cm_spec],
        out_shape=[jax.ShapeDtypeStruct((nset, 1, S5_SET_STATE), F32)] * 2
        + [jax.ShapeDtypeStruct((nset, S5_SET_IN, 2 * S5_SET_STATE), BF16)]
        + [jax.ShapeDtypeStruct((nset, S5_SET_STATE, S5_SET_IN), BF16)] * 2,
        compiler_params=_params(("parallel",)),
        name="s5_operands",
    )(lane_row(a_re), lane_row(a_im), step_row, b_rows(b_re), b_rows(b_im), c_rows(c_re), c_rows(c_im))
    lam_r = lb_re.reshape(nset, STATE_TILES, LANES)
    lam_i = lb_im.reshape(nset, STATE_TILES, LANES)
    return bmat, cmr, cmi, lam_r, lam_i


def _s5_scan_body(u_ref, bmat_ref, cre_ref, cim_ref, lr_ref, li_ref, d_ref, o_ref, slab_ref, carry_ref):
    c = pl.program_id(1)
    nb, ts, _ = u_ref.shape
    nsets = bmat_ref.shape[0]
    chains = [(ss, b) for ss in range(nsets) for b in range(nb)]

    @pl.when(c == 0)
    def _():
        carry_ref[...] = jnp.zeros_like(carry_ref)

    def slab_base(chain, tile):
        return (chain * 2 * STATE_TILES + tile) * SLAB_PITCH

    def set_cols(ss):
        return slice(ss * S5_SET_IN, (ss + 1) * S5_SET_IN)

    for chain, (ss, b) in enumerate(chains):
        bu = jnp.dot(u_ref[b, :, set_cols(ss)].astype(BF16), bmat_ref[ss], preferred_element_type=F32)
        for tile in range(2 * STATE_TILES):
            slab_ref[pl.ds(slab_base(chain, tile), ts), :] = bu[:, tile * LANES:(tile + 1) * LANES]

    lam = [(lr_ref[ss], li_ref[ss]) for ss in range(nsets)]

    def step(t, state):
        new_state = []
        for chain, (ss, b) in enumerate(chains):
            lam_r, lam_i = lam[ss]
            xr, xi = state[2 * chain], state[2 * chain + 1]
            re_rows = pl.ds(slab_base(chain, 0) + t, STATE_TILES, stride=SLAB_PITCH)
            im_rows = pl.ds(slab_base(chain, STATE_TILES) + t, STATE_TILES, stride=SLAB_PITCH)
            nxr = lam_r * xr - lam_i * xi + slab_ref[re_rows, :]
            nxi = lam_r * xi + lam_i * xr + slab_ref[im_rows, :]
            slab_ref[re_rows, :] = nxr
            slab_ref[im_rows, :] = nxi
            new_state += [nxr, nxi]
        return tuple(new_state)

    init = tuple(carry_ref[s] for s in range(2 * len(chains)))
    final = lax.fori_loop(0, ts, step, init, unroll=8)
    for s in range(2 * len(chains)):
        carry_ref[s] = final[s]

    for chain, (ss, b) in enumerate(chains):
        xr = jnp.concatenate(
            [slab_ref[pl.ds(slab_base(chain, tile), ts), :] for tile in range(STATE_TILES)], axis=1)
        xi = jnp.concatenate(
            [slab_ref[pl.ds(slab_base(chain, STATE_TILES + tile), ts), :] for tile in range(STATE_TILES)], axis=1)
        y = (jnp.dot(xr.astype(BF16), cre_ref[ss], preferred_element_type=F32)
             - jnp.dot(xi.astype(BF16), cim_ref[ss], preferred_element_type=F32))
        y = y + d_ref[:, set_cols(ss)] * u_ref[b, :, set_cols(ss)]
        o_ref[b, :, set_cols(ss)] = jax.nn.gelu(y).astype(o_ref.dtype)


def _s5_scan(u3, bmat, cre, cim, lam_r, lam_i, d_skip):
    nb, lp, d = u3.shape
    nset = bmat.shape[0]
    per = S5_SETS_PER_STEP
    cols = per * S5_SET_IN
    chains = per * nb
    return pl.pallas_call(
        _s5_scan_body,
        grid=(nset // per, lp // SEQ_BLOCK),
        in_specs=[
            pl.BlockSpec((nb, SEQ_BLOCK, cols), lambda s, c: (0, c, s)),
            pl.BlockSpec((per, S5_SET_IN, 2 * S5_SET_STATE), lambda s, c: (s, 0, 0)),
            pl.BlockSpec((per, S5_SET_STATE, S5_SET_IN), lambda s, c: (s, 0, 0)),
            pl.BlockSpec((per, S5_SET_STATE, S5_SET_IN), lambda s, c: (s, 0, 0)),
            pl.BlockSpec((per, STATE_TILES, LANES), lambda s, c: (s, 0, 0)),
            pl.BlockSpec((per, STATE_TILES, LANES), lambda s, c: (s, 0, 0)),
            pl.BlockSpec((1, cols), lambda s, c: (0, s)),
        ],
        out_specs=pl.BlockSpec((nb, SEQ_BLOCK, cols), lambda s, c: (0, c, s)),
        out_shape=jax.ShapeDtypeStruct((nb, lp, d), BF16),
        scratch_shapes=[pltpu.VMEM((chains * 2 * STATE_TILES * SLAB_PITCH, LANES), F32),
                        pltpu.VMEM((2 * chains, STATE_TILES, LANES), F32)],
        compiler_params=_params(("parallel", "arbitrary")),
        name="s5_scan",
    )(u3, bmat, cre, cim, lam_r, lam_i, d_skip.reshape(1, d))


def _attn_conv_layer(h, g, w_proj, w_o, layer, b_f, conv_w, conv_b, batch, lp, pad):
    u = _rmsnorm(h, g[0], BF16)
    qkv_w = 3 * ATTN_WIDTH
    qkv = _matmul_slabs(u, w_proj, layer, 0, qkv_w, BF16, 768, "qkv_proj")
    gates = _matmul_slabs(u, w_proj, layer, qkv_w, qkv_w, F32, 768, "gate_proj")
    fg = _matmul_slabs(u, w_proj, layer, 2 * qkv_w, LANES, F32, LANES, "forget_proj")[0]
    b_pad = jnp.pad(b_f, (0, LANES - ATTN_HEADS)).reshape(1, LANES)
    negc = _neg_cum_logf(fg, b_pad, batch, lp, pad)
    negc = jnp.transpose(negc[:, :ATTN_HEADS].reshape(batch, lp // SEQ_BLOCK, SEQ_BLOCK, HEAD_PAIRS, 2),
                         (0, 3, 1, 4, 2))
    attn = _fox_attn(qkv, negc, batch, lp)
    conv = _gated_conv(gates, conv_w, conv_b)
    return _out_proj_residual(attn, conv, w_o, layer, h, g[1])


def _s5_layer(h, g, a_re, a_im, log_step, b_re, b_im, c_re, c_im, d_skip, w_glu1, w_glu2, layer, batch, lp):
    m, d = h.shape
    u = _rmsnorm(h, g[0], F32)
    operands = _s5_operands(a_re, a_im, log_step, b_re, b_im, c_re, c_im)
    act = _s5_scan(u.reshape(batch, lp, d), *operands, d_skip)
    return _glu_residual(act.reshape(m, d), w_glu1, w_glu2, layer, h, g[1])


def _projection_weights(w_in):
    qkv_w = 3 * ATTN_WIDTH
    fg = w_in[..., qkv_w:qkv_w + ATTN_HEADS]
    fill = jnp.zeros(fg.shape[:-1] + (LANES - ATTN_HEADS,), w_in.dtype)
    return jnp.concatenate([w_in[..., :qkv_w], w_in[..., qkv_w + ATTN_HEADS:], fg, fill], axis=-1).astype(BF16)


def kernel(x, meta_tokens, norm_g, ab_w_in, ab_b_f, ab_conv_w, ab_conv_b, ab_w_o, s5_a_re, s5_a_im, s5_log_step, s5_b_re, s5_b_im, s5_c_re, s5_c_im, s5_d, s5_w_glu1, s5_w_glu2, ffn_w_gate, ffn_w_up, ffn_w_down):
    batch, seq, d = x.shape
    depth = norm_g.shape[0]
    tokens = N_META + seq
    lp = pl.cdiv(tokens, SEQ_BLOCK) * SEQ_BLOCK
    assert (batch * lp) % ROW_BLOCK == 0
    pad = lp - tokens
    meta = jnp.broadcast_to(meta_tokens.astype(x.dtype)[None], (batch, N_META, d))
    h = jnp.concatenate([jnp.zeros((batch, pad, d), x.dtype), meta, x], axis=1).reshape(batch * lp, d)
    w_proj = _projection_weights(ab_w_in)
    w_o = ab_w_o.astype(BF16)
    w_glu1 = s5_w_glu1.astype(BF16)
    w_glu2 = s5_w_glu2.astype(BF16)
    w_gate = ffn_w_gate.astype(BF16)
    w_up = ffn_w_up.astype(BF16)
    w_down = ffn_w_down.astype(BF16)
    for i in range(depth):
        g = norm_g[i]
        j = i // 2
        if i % 2 == 0:
            h = _attn_conv_layer(h, g, w_proj, w_o, j, ab_b_f[j], ab_conv_w[j], ab_conv_b[j], batch, lp, pad)
        else:
            h = _s5_layer(h, g, s5_a_re[j], s5_a_im[j], s5_log_step[j], s5_b_re[j], s5_b_im[j],
                          s5_c_re[j], s5_c_im[j], s5_d[j], w_glu1, w_glu2, j, batch, lp)
        h = _ffn_residual(h, g[2], g[3], w_gate, w_up, w_down, i)
    return h.reshape(batch, lp, d)[:, pad + N_META:]
```

```python
import functools
import math

import jax
import jax.numpy as jnp
from jax import lax
from jax.experimental import pallas as pl
from jax.experimental.pallas import tpu as pltpu

F32 = jnp.float32
BF16 = jnp.bfloat16

N_META = 16
ATTN_HEADS = 16
HEAD_DIM = 64
ATTN_WIDTH = ATTN_HEADS * HEAD_DIM
CONV_K = 3
S5_GROUP = 16
S5_STATE = 64
S5_MIN_DECAY = 1e-4
NORM_EPS = 1e-6

LANES = 128
SUBLANES = 8
ROW_BLOCK = 768
SEQ_BLOCK = 384
HEAD_PAIRS = ATTN_HEADS * HEAD_DIM // LANES
S5_SET_GROUPS = 16
S5_SET_IN = S5_SET_GROUPS * S5_GROUP
S5_SET_STATE = S5_SET_GROUPS * S5_STATE
S5_SETS_PER_STEP = 2
STATE_TILES = S5_SET_STATE // LANES
SLAB_PITCH = SEQ_BLOCK + SUBLANES
KEY_MASK = -1e30
VMEM_LIMIT = 60 * 1024 * 1024


def _params(sem):
    return pltpu.CompilerParams(dimension_semantics=sem, vmem_limit_bytes=VMEM_LIMIT)


def _rmsnorm_body(x_ref, g_ref, o_ref):
    x = x_ref[...]
    ms = jnp.mean(x * x, axis=-1, keepdims=True)
    o_ref[...] = (x * lax.rsqrt(ms + NORM_EPS) * g_ref[...]).astype(o_ref.dtype)


def _rmsnorm(h, g, out_dtype):
    m, d = h.shape
    return pl.pallas_call(
        _rmsnorm_body,
        grid=(m // ROW_BLOCK,),
        in_specs=[pl.BlockSpec((ROW_BLOCK, d), lambda i: (i, 0)),
                  pl.BlockSpec((1, d), lambda i: (0, 0))],
        out_specs=pl.BlockSpec((ROW_BLOCK, d), lambda i: (i, 0)),
        out_shape=jax.ShapeDtypeStruct((m, d), out_dtype),
        compiler_params=_params(("parallel",)),
        name="rmsnorm",
    )(h, g.reshape(1, d))


def _matmul_slabs_body(x_ref, w_ref, o_ref, *, nslab):
    r = jnp.dot(x_ref[...], w_ref[...], preferred_element_type=F32)
    for c in range(nslab):
        o_ref[c] = r[:, c * LANES:(c + 1) * LANES].astype(o_ref.dtype)


def _matmul_slabs(x, w, layer, col0, n, out_dtype, tn, name):
    m, k = x.shape
    nslab = tn // LANES
    assert col0 % tn == 0 and n % tn == 0
    first = col0 // tn
    return pl.pallas_call(
        functools.partial(_matmul_slabs_body, nslab=nslab),
        grid=(m // ROW_BLOCK, n // tn),
        in_specs=[pl.BlockSpec((ROW_BLOCK, k), lambda i, j: (i, 0)),
                  pl.BlockSpec((None, k, tn), lambda i, j: (layer, 0, first + j))],
        out_specs=pl.BlockSpec((nslab, ROW_BLOCK, LANES), lambda i, j: (j, i, 0)),
        out_shape=jax.ShapeDtypeStruct((n // LANES, m, LANES), out_dtype),
        compiler_params=_params(("parallel", "parallel")),
        name=name,
    )(x, w)


def _neg_cum_logf_body(fg_ref, b_ref, o_ref, carry_ref, *, pad):
    c = pl.program_id(1)

    @pl.when(c == 0)
    def _():
        carry_ref[...] = jnp.zeros_like(carry_ref)

    z = fg_ref[...] + b_ref[...]
    logf = jnp.minimum(z, 0.0) - jnp.log1p(jnp.exp(-jnp.abs(z)))
    t = logf.shape[0]
    row = lax.broadcasted_iota(jnp.int32, (t, t), 0)
    col = lax.broadcasted_iota(jnp.int32, (t, t), 1)
    tri = (col <= row).astype(BF16)
    hi = logf.astype(BF16)
    rem = logf - hi.astype(F32)
    mid = rem.astype(BF16)
    lo = (rem - mid.astype(F32)).astype(BF16)
    cs = (jnp.dot(tri, hi, preferred_element_type=F32)
          + jnp.dot(tri, mid, preferred_element_type=F32)
          + jnp.dot(tri, lo, preferred_element_type=F32)) + carry_ref[...]
    carry_ref[...] = cs[t - 1:t, :]
    pos = c * t + lax.broadcasted_iota(jnp.int32, (t, 1), 0)
    o_ref[...] = jnp.where(pos < pad, KEY_MASK, -cs)


def _neg_cum_logf(fg, b_f, batch, lp, pad):
    m = fg.shape[0]
    nblk = lp // SEQ_BLOCK
    return pl.pallas_call(
        functools.partial(_neg_cum_logf_body, pad=pad),
        grid=(batch, nblk),
        in_specs=[pl.BlockSpec((SEQ_BLOCK, LANES), lambda b, c: (b * nblk + c, 0)),
                  pl.BlockSpec((1, LANES), lambda b, c: (0, 0))],
        out_specs=pl.BlockSpec((SEQ_BLOCK, LANES), lambda b, c: (b * nblk + c, 0)),
        out_shape=jax.ShapeDtypeStruct((m, LANES), F32),
        scratch_shapes=[pltpu.VMEM((1, LANES), F32)],
        compiler_params=_params(("parallel", "arbitrary")),
        name="neg_cum_logf",
    )(fg, b_f)


def _fox_attn_body(q_ref, k_ref, v_ref, nc_ref, o_ref, qh_ref, vaug_ref, s_ref, p_ref, scale_ref, m_ref, acc_ref):
    qi = pl.program_id(2)
    tq = q_ref.shape[0]
    nblk = vaug_ref.shape[0]
    lane = lax.broadcasted_iota(jnp.int32, (1, LANES), 1)
    first_head = lane < HEAD_DIM

    @pl.when(qi == 0)
    def _():
        ones_a = jnp.broadcast_to((lane == 0).astype(BF16), (tq, LANES))
        ones_b = jnp.broadcast_to((lane == 1).astype(BF16), (tq, LANES))
        for kb in range(nblk):
            v = v_ref[kb * tq:(kb + 1) * tq, :]
            zero = jnp.zeros_like(v)
            vaug_ref[kb, 0:tq, 0:LANES] = jnp.where(first_head, v, zero)
            vaug_ref[kb, tq:2 * tq, 0:LANES] = jnp.where(first_head, zero, v)
            vaug_ref[kb, 0:tq, LANES:2 * LANES] = ones_a
            vaug_ref[kb, tq:2 * tq, LANES:2 * LANES] = ones_b

    q = q_ref[...]
    zero = jnp.zeros_like(q)
    qh_ref[0:tq, :] = jnp.where(first_head, q, zero) * (HEAD_DIM ** -0.5)
    qh_ref[tq:2 * tq, :] = jnp.where(first_head, zero, q) * (HEAD_DIM ** -0.5)
    m_ref[...] = jnp.full_like(m_ref, -jnp.inf)
    acc_ref[...] = jnp.zeros_like(acc_ref)

    def scores(ki, slot):
        rows = pl.ds(pl.multiple_of(ki * tq, tq), tq)
        s = lax.dot_general(qh_ref[...], k_ref[rows, :], (((1,), (1,)), ((), ())),
                            preferred_element_type=F32)
        nc = nc_ref[ki]
        s_ref[slot, 0:tq, :] = s[0:tq] + nc[0:1, :]
        s_ref[slot, tq:2 * tq, :] = s[tq:2 * tq] + nc[1:2, :]

    def softmax(slot, diagonal):
        s = s_ref[slot]
        if diagonal:
            row = lax.broadcasted_iota(jnp.int32, (2 * tq, tq), 0)
            row = jnp.where(row >= tq, row - tq, row)
            col = lax.broadcasted_iota(jnp.int32, (2 * tq, tq), 1)
            s = jnp.where(col <= row, s, -jnp.inf)
        m_prev = m_ref[...]
        m_next = jnp.maximum(m_prev, jnp.max(s, axis=-1, keepdims=True))
        m_ref[...] = m_next
        alpha = jnp.exp(m_prev - m_next)
        p = jnp.exp(s - jnp.concatenate([m_next] * (tq // LANES), axis=1)).astype(BF16)
        p_ref[slot, :, 0:tq] = p[0:tq]
        p_ref[slot, :, tq:2 * tq] = p[tq:2 * tq]
        alpha_a = alpha[0:tq]
        alpha_b = alpha[tq:2 * tq]
        scale_ref[slot, :, 0:LANES] = jnp.where(first_head, alpha_a, alpha_b)
        scale_ref[slot, :, LANES:2 * LANES] = jnp.where(lane == 0, alpha_a, alpha_b)

    def accumulate(ki, slot):
        pv = jnp.dot(p_ref[slot], vaug_ref[ki], preferred_element_type=F32)
        acc_ref[...] = scale_ref[slot] * acc_ref[...] + pv

    scores(qi, 0)
    softmax(0, True)
    scores(0, 1)

    def visible_pair(j, carry):
        accumulate(jnp.where(j == 0, qi, 2 * j - 1), 0)
        softmax(1, False)
        scores(2 * j + 1, 0)
        accumulate(2 * j, 1)
        softmax(0, False)
        scores(2 * j + 2, 1)
        return carry

    npairs = qi // 2
    lax.fori_loop(0, npairs, visible_pair, 0)
    accumulate(jnp.where(npairs == 0, qi, 2 * npairs - 1), 0)

    @pl.when(lax.rem(qi, 2) == 1)
    def _():
        softmax(1, False)
        accumulate(qi - 1, 1)

    acc = acc_ref[...]
    sum_a = jnp.broadcast_to(acc[:, LANES:LANES + 1], (tq, LANES))
    sum_b = jnp.broadcast_to(acc[:, LANES + 1:LANES + 2], (tq, LANES))
    o_ref[...] = (acc[:, 0:LANES] / jnp.where(first_head, sum_a, sum_b)).astype(o_ref.dtype)


def _fox_attn(qkv, negc, batch, lp):
    nslab, m, _ = qkv.shape
    nq = lp // SEQ_BLOCK
    qkv_seq = qkv.reshape(nslab, batch, lp, LANES)
    seq_blk = (None, None, lp, LANES)
    return pl.pallas_call(
        _fox_attn_body,
        grid=(batch, HEAD_PAIRS, nq),
        in_specs=[
            pl.BlockSpec((None, SEQ_BLOCK, LANES), lambda b, p, qi: (p, b * nq + qi, 0)),
            pl.BlockSpec(seq_blk, lambda b, p, qi: (HEAD_PAIRS + p, b, 0, 0)),
            pl.BlockSpec(seq_blk, lambda b, p, qi: (2 * HEAD_PAIRS + p, b, 0, 0)),
            pl.BlockSpec((None, None, nq, 2, SEQ_BLOCK), lambda b, p, qi: (b, p, 0, 0, 0)),
        ],
        out_specs=pl.BlockSpec((SEQ_BLOCK, LANES), lambda b, p, qi: (b * nq + qi, p)),
        out_shape=jax.ShapeDtypeStruct((m, ATTN_WIDTH), BF16),
        scratch_shapes=[pltpu.VMEM((2 * SEQ_BLOCK, LANES), BF16),
                        pltpu.VMEM((nq, 2 * SEQ_BLOCK, 2 * LANES), BF16),
                        pltpu.VMEM((2, 2 * SEQ_BLOCK, SEQ_BLOCK), F32),
                        pltpu.VMEM((2, SEQ_BLOCK, 2 * SEQ_BLOCK), BF16),
                        pltpu.VMEM((2, SEQ_BLOCK, 2 * LANES), F32),
                        pltpu.VMEM((2 * SEQ_BLOCK, LANES), F32),
                        pltpu.VMEM((SEQ_BLOCK, 2 * LANES), F32)],
        compiler_params=_params(("parallel", "parallel", "arbitrary")),
        name="fox_attn",
    )(qkv, qkv_seq, qkv_seq, negc)


CONV_HALO = 16


def _gate_conv_body(x_ref, xh_ref, w_ref, cw_ref, cb_ref, o_ref, xcat_ref, z_ref):
    i = pl.program_id(0)
    j = pl.program_id(1)
    tm = x_ref.shape[0]
    nch = o_ref.shape[1]

    @pl.when(j == 0)
    def _():
        xcat_ref[0:CONV_HALO, :] = xh_ref[...]
        xcat_ref[CONV_HALO:CONV_HALO + tm, :] = x_ref[...]

    r = jnp.dot(xcat_ref[...], w_ref[...], preferred_element_type=F32)
    z_all = r[:, nch:2 * nch] * r[:, 2 * nch:3 * nch]
    z_ref[...] = z_all
    head = z_all[0:CONV_HALO]
    z_ref[0:CONV_HALO, :] = jnp.where(i > 0, head, jnp.zeros_like(head))
    z = z_all[CONV_HALO:CONV_HALO + tm]
    z1 = z_ref[CONV_HALO - 1:CONV_HALO - 1 + tm, :]
    z2 = z_ref[CONV_HALO - 2:CONV_HALO - 2 + tm, :]
    conv = cw_ref[0:1, :] * z2 + cw_ref[1:2, :] * z1 + cw_ref[2:3, :] * z + cb_ref[...]
    o_ref[...] = (r[CONV_HALO:CONV_HALO + tm, 0:nch] * conv).astype(o_ref.dtype)


def _gate_conv(x, w, layer, conv_w, conv_b, nch=256):
    m, k = x.shape
    channels = w.shape[2] // 3
    halo_per_block = ROW_BLOCK // CONV_HALO
    return pl.pallas_call(
        _gate_conv_body,
        grid=(m // ROW_BLOCK, channels // nch),
        in_specs=[
            pl.BlockSpec((ROW_BLOCK, k), lambda i, j: (i, 0)),
            pl.BlockSpec((CONV_HALO, k), lambda i, j: (jnp.maximum(i * halo_per_block - 1, 0), 0)),
            pl.BlockSpec((None, k, 3 * nch), lambda i, j: (layer, 0, j)),
            pl.BlockSpec((CONV_K, nch), lambda i, j: (0, j)),
            pl.BlockSpec((1, nch), lambda i, j: (0, j)),
        ],
        out_specs=pl.BlockSpec((ROW_BLOCK, nch), lambda i, j: (i, j)),
        out_shape=jax.ShapeDtypeStruct((m, channels), BF16),
        scratch_shapes=[pltpu.VMEM((ROW_BLOCK + CONV_HALO, k), BF16),
                        pltpu.VMEM((ROW_BLOCK + CONV_HALO, nch), F32)],
        compiler_params=_params(("parallel", "arbitrary")),
        name="gate_conv",
    )(x, x, w, conv_w, conv_b.reshape(1, -1))


def _residual_norm_store(m_ref, h_ref, g_ref, o_ref):
    nchunk, _, tn = m_ref.shape
    ss = None
    for c in range(nchunk):
        mc = m_ref[c]
        part = jnp.sum(mc * mc, axis=-1, keepdims=True)
        ss = part if ss is None else ss + part
    scale = lax.rsqrt(ss / (nchunk * tn) + NORM_EPS)
    for c in range(nchunk):
        cols = slice(c * tn, (c + 1) * tn)
        o_ref[:, cols] = h_ref[:, cols] + m_ref[c] * scale * g_ref[:, cols]


def _out_proj_body(xa_ref, xb_ref, w_ref, h_ref, g_ref, o_ref, m_ref):
    nchunk, _, tn = m_ref.shape
    ka = xa_ref.shape[1]
    xa = xa_ref[...]
    xb = xb_ref[...]
    for c in range(nchunk):
        cols = slice(c * tn, (c + 1) * tn)
        m_ref[c] = (jnp.dot(xa, w_ref[0:ka, cols], preferred_element_type=F32)
                    + jnp.dot(xb, w_ref[ka:, cols], preferred_element_type=F32))
    _residual_norm_store(m_ref, h_ref, g_ref, o_ref)


def _out_proj_residual(xa, xb, w, layer, h, g, tn=512):
    m, ka = xa.shape
    kb = xb.shape[1]
    n = w.shape[2]
    return pl.pallas_call(
        _out_proj_body,
        grid=(m // ROW_BLOCK,),
        in_specs=[
            pl.BlockSpec((ROW_BLOCK, ka), lambda i: (i, 0)),
            pl.BlockSpec((ROW_BLOCK, kb), lambda i: (i, 0)),
            pl.BlockSpec((None, ka + kb, n), lambda i: (layer, 0, 0)),
            pl.BlockSpec((ROW_BLOCK, n), lambda i: (i, 0)),
            pl.BlockSpec((1, n), lambda i: (0, 0)),
        ],
        out_specs=pl.BlockSpec((ROW_BLOCK, n), lambda i: (i, 0)),
        out_shape=jax.ShapeDtypeStruct((m, n), F32),
        scratch_shapes=[pltpu.VMEM((n // tn, ROW_BLOCK, tn), F32)],
        compiler_params=_params(("parallel",)),
        name="out_proj_residual",
    )(xa, xb, w, h, g.reshape(1, n))


def _glu_body(x_ref, w1_ref, w2_ref, h_ref, g_ref, o_ref, m_ref):
    j = pl.program_id(1)
    x = x_ref[...]
    a = jnp.dot(x, w1_ref[...], preferred_element_type=F32)
    b = jnp.dot(x, w2_ref[...], preferred_element_type=F32)
    m_ref[j] = a * jax.nn.sigmoid(b)

    @pl.when(j == pl.num_programs(1) - 1)
    def _():
        _residual_norm_store(m_ref, h_ref, g_ref, o_ref)


def _glu_residual(x, w1, w2, layer, h, g, tn=512):
    m, k = x.shape
    n = w1.shape[2]
    return pl.pallas_call(
        _glu_body,
        grid=(m // ROW_BLOCK, n // tn),
        in_specs=[
            pl.BlockSpec((ROW_BLOCK, k), lambda i, j: (i, 0)),
            pl.BlockSpec((None, k, tn), lambda i, j: (layer, 0, j)),
            pl.BlockSpec((None, k, tn), lambda i, j: (layer, 0, j)),
            pl.BlockSpec((ROW_BLOCK, n), lambda i, j: (i, 0)),
            pl.BlockSpec((1, n), lambda i, j: (0, 0)),
        ],
        out_specs=pl.BlockSpec((ROW_BLOCK, n), lambda i, j: (i, 0)),
        out_shape=jax.ShapeDtypeStruct((m, n), F32),
        scratch_shapes=[pltpu.VMEM((n // tn, ROW_BLOCK, tn), F32)],
        compiler_params=_params(("parallel", "arbitrary")),
        name="glu_residual",
    )(x, w1, w2, h, g.reshape(1, n))


def _ffn_body(h_ref, gin_ref, gout_ref, wg_ref, wu_ref, wd_ref, o_ref, *rest, emit_inv_rms):
    inv_ref = rest[0] if emit_inv_rms else None
    hn_ref, acc_ref = rest[-2:]
    j = pl.program_id(1)

    @pl.when(j == 0)
    def _():
        x = h_ref[...]
        ms = jnp.mean(x * x, axis=-1, keepdims=True)
        hn_ref[...] = (x * lax.rsqrt(ms + NORM_EPS) * gin_ref[...]).astype(hn_ref.dtype)
        acc_ref[...] = jnp.zeros_like(acc_ref)

    hn = hn_ref[...]
    gate = jnp.dot(hn, wg_ref[...], preferred_element_type=F32)
    up = jnp.dot(hn, wu_ref[...], preferred_element_type=F32)
    act = (gate * jax.nn.sigmoid(gate) * up).astype(BF16)
    acc_ref[...] += jnp.dot(act, wd_ref[...], preferred_element_type=F32)

    @pl.when(j == pl.num_programs(1) - 1)
    def _():
        f = acc_ref[...]
        ms = jnp.mean(f * f, axis=-1, keepdims=True)
        out = h_ref[...] + f * lax.rsqrt(ms + NORM_EPS) * gout_ref[...]
        o_ref[...] = out
        if emit_inv_rms:
            ms_out = jnp.mean(out * out, axis=-1, keepdims=True)
            inv_ref[...] = jnp.broadcast_to(lax.rsqrt(ms_out + NORM_EPS), inv_ref.shape)


def _ffn_residual(h, g_in, g_out, w_gate, w_up, w_down, layer, emit_inv_rms, th=512):
    m, d = h.shape
    hidden = w_gate.shape[2]
    row_spec = pl.BlockSpec((ROW_BLOCK, d), lambda i, j: (i, 0))
    out_specs = [row_spec]
    out_shape = [jax.ShapeDtypeStruct((m, d), F32)]
    if emit_inv_rms:
        out_specs.append(pl.BlockSpec((ROW_BLOCK, LANES), lambda i, j: (i, 0)))
        out_shape.append(jax.ShapeDtypeStruct((m, LANES), F32))
    outs = pl.pallas_call(
        functools.partial(_ffn_body, emit_inv_rms=emit_inv_rms),
        grid=(m // ROW_BLOCK, hidden // th),
        in_specs=[
            row_spec,
            pl.BlockSpec((1, d), lambda i, j: (0, 0)),
            pl.BlockSpec((1, d), lambda i, j: (0, 0)),
            pl.BlockSpec((None, d, th), lambda i, j: (layer, 0, j)),
            pl.BlockSpec((None, d, th), lambda i, j: (layer, 0, j)),
            pl.BlockSpec((None, th, d), lambda i, j: (layer, j, 0)),
        ],
        out_specs=out_specs,
        out_shape=out_shape,
        scratch_shapes=[pltpu.VMEM((ROW_BLOCK, d), BF16),
                        pltpu.VMEM((ROW_BLOCK, d), F32)],
        compiler_params=_params(("parallel", "arbitrary")),
        name="ffn_residual",
    )(h, g_in.reshape(1, d), g_out.reshape(1, d), w_gate, w_up, w_down)
    return tuple(outs) if emit_inv_rms else outs[0]


def _lane_repeat(x, reps, exact_f32):
    n = x.shape[1]
    row = lax.broadcasted_iota(jnp.int32, (n, n * reps), 0)
    col = lax.broadcasted_iota(jnp.int32, (n, n * reps), 1)
    sel = (jnp.bitwise_and(col, n - 1) == row).astype(BF16)
    hi = x.astype(BF16)
    out = jnp.dot(hi, sel, preferred_element_type=F32)
    if exact_f32:
        rem = x - hi.astype(F32)
        mid = rem.astype(BF16)
        lo = (rem - mid.astype(F32)).astype(BF16)
        out = out + jnp.dot(mid, sel, preferred_element_type=F32) + jnp.dot(lo, sel, preferred_element_type=F32)
    return out


def _diag_block_mask(shape, row_shift, col_shift):
    row = lax.broadcasted_iota(jnp.int32, shape, 0)
    col = lax.broadcasted_iota(jnp.int32, shape, 1)
    return jnp.right_shift(row, row_shift) == jnp.right_shift(col, col_shift)


def _s5_operands_body(are_ref, aim_ref, ls_ref, bre_ref, bim_ref, cre_ref, cim_ref,
                      lbr_ref, lbi_ref, bmat_ref, cmr_ref, cmi_ref):
    lam_re = jnp.minimum(are_ref[...], -S5_MIN_DECAY)
    lam_im = aim_ref[...]
    delta = jnp.exp(ls_ref[...])
    mag = jnp.exp(lam_re * delta)
    ang = lam_im * delta
    lb_re = mag * jnp.cos(ang)
    lb_im = mag * jnp.sin(ang)
    den = lam_re * lam_re + lam_im * lam_im
    nr = lb_re - 1.0
    ni = lb_im
    coef_re = (nr * lam_re + ni * lam_im) / den
    coef_im = (ni * lam_re - nr * lam_im) / den
    lbr_ref[...] = lb_re
    lbi_ref[...] = lb_im

    br = _lane_repeat(bre_ref[...], S5_SET_GROUPS, True)
    bi = _lane_repeat(bim_ref[...], S5_SET_GROUPS, True)
    in_mask = _diag_block_mask(br.shape, S5_GROUP.bit_length() - 1, S5_STATE.bit_length() - 1)
    zero = jnp.zeros_like(br)
    bmat_ref[:, 0:S5_SET_STATE] = jnp.where(in_mask, coef_re * br - coef_im * bi, zero).astype(bmat_ref.dtype)
    bmat_ref[:, S5_SET_STATE:2 * S5_SET_STATE] = jnp.where(
        in_mask, coef_re * bi + coef_im * br, zero).astype(bmat_ref.dtype)

    cr = _lane_repeat(cre_ref[...], S5_SET_GROUPS, False)
    ci = _lane_repeat(cim_ref[...], S5_SET_GROUPS, False)
    out_mask = _diag_block_mask(cr.shape, S5_STATE.bit_length() - 1, S5_GROUP.bit_length() - 1)
    cmr_ref[...] = jnp.where(out_mask, cr, jnp.zeros_like(cr)).astype(cmr_ref.dtype)
    cmi_ref[...] = jnp.where(out_mask, ci, jnp.zeros_like(ci)).astype(cmi_ref.dtype)


def _s5_operands(a_re, a_im, log_step, b_re, b_im, c_re, c_im):
    groups, state = a_re.shape
    nset = groups // S5_SET_GROUPS
    lane_row = lambda a: a.reshape(nset, 1, S5_SET_STATE)
    step_row = lane_row(jnp.broadcast_to(log_step[:, None], (groups, state)))
    b_rows = lambda b: jnp.transpose(b, (0, 2, 1)).reshape(nset, S5_SET_IN, state)
    c_rows = lambda c: jnp.transpose(c, (0, 2, 1)).reshape(nset, S5_SET_STATE, S5_GROUP)
    row_spec = pl.BlockSpec((None, 1, S5_SET_STATE), lambda s: (s, 0, 0))
    b_spec = pl.BlockSpec((None, S5_SET_IN, state), lambda s: (s, 0, 0))
    c_spec = pl.BlockSpec((None, S5_SET_STATE, S5_GROUP), lambda s: (s, 0, 0))
    cm_spec = pl.BlockSpec((None, S5_SET_STATE, S5_SET_IN), lambda s: (s, 0, 0))
    lb_re, lb_im, bmat, cmr, cmi = pl.pallas_call(
        _s5_operands_body,
        grid=(nset,),
        in_specs=[row_spec, row_spec, row_spec, b_spec, b_spec, c_spec, c_spec],
        out_specs=[row_spec, row_spec,
                   pl.BlockSpec((None, S5_SET_IN, 2 * S5_SET_STATE), lambda s: (s, 0, 0)), cm_spec, cm_spec],
        out_shape=[jax.ShapeDtypeStruct((nset, 1, S5_SET_STATE), F32)] * 2
        + [jax.ShapeDtypeStruct((nset, S5_SET_IN, 2 * S5_SET_STATE), BF16)]
        + [jax.ShapeDtypeStruct((nset, S5_SET_STATE, S5_SET_IN), BF16)] * 2,
        compiler_params=_params(("parallel",)),
        name="s5_operands",
    )(lane_row(a_re), lane_row(a_im), step_row, b_rows(b_re), b_rows(b_im), c_rows(c_re), c_rows(c_im))
    lam_r = lb_re.reshape(nset, STATE_TILES, LANES)
    lam_i = lb_im.reshape(nset, STATE_TILES, LANES)
    return bmat, cmr, cmi, lam_r, lam_i


def _s5_scan_body(h_ref, inv_ref, g_ref, bmat_ref, cre_ref, cim_ref, lr_ref, li_ref, d_ref, o_ref, slab_ref,
                  carry_ref):
    c = pl.program_id(1)
    nb, ts, _ = h_ref.shape
    nsets = bmat_ref.shape[0]
    chains = [(ss, b) for ss in range(nsets) for b in range(nb)]

    @pl.when(c == 0)
    def _():
        carry_ref[...] = jnp.zeros_like(carry_ref)

    def slab_base(chain, tile):
        return (chain * 2 * STATE_TILES + tile) * SLAB_PITCH

    def set_cols(ss):
        return slice(ss * S5_SET_IN, (ss + 1) * S5_SET_IN)

    def normed(ss, b):
        inv = jnp.concatenate([inv_ref[b]] * (S5_SET_IN // LANES), axis=1)
        return h_ref[b, :, set_cols(ss)] * inv * g_ref[:, set_cols(ss)]

    for chain, (ss, b) in enumerate(chains):
        bu = jnp.dot(normed(ss, b).astype(BF16), bmat_ref[ss], preferred_element_type=F32)
        for tile in range(2 * STATE_TILES):
            slab_ref[pl.ds(slab_base(chain, tile), ts), :] = bu[:, tile * LANES:(tile + 1) * LANES]

    lam = [(lr_ref[ss], li_ref[ss]) for ss in range(nsets)]

    def step(t, state):
        new_state = []
        for chain, (ss, b) in enumerate(chains):
            lam_r, lam_i = lam[ss]
            xr, xi = state[2 * chain], state[2 * chain + 1]
            re_rows = pl.ds(slab_base(chain, 0) + t, STATE_TILES, stride=SLAB_PITCH)
            im_rows = pl.ds(slab_base(chain, STATE_TILES) + t, STATE_TILES, stride=SLAB_PITCH)
            nxr = lam_r * xr - lam_i * xi + slab_ref[re_rows, :]
            nxi = lam_r * xi + lam_i * xr + slab_ref[im_rows, :]
            slab_ref[re_rows, :] = nxr
            slab_ref[im_rows, :] = nxi
            new_state += [nxr, nxi]
        return tuple(new_state)

    init = tuple(carry_ref[s] for s in range(2 * len(chains)))
    final = lax.fori_loop(0, ts, step, init, unroll=8)
    for s in range(2 * len(chains)):
        carry_ref[s] = final[s]

    for chain, (ss, b) in enumerate(chains):
        xr = jnp.concatenate(
            [slab_ref[pl.ds(slab_base(chain, tile), ts), :] for tile in range(STATE_TILES)], axis=1)
        xi = jnp.concatenate(
            [slab_ref[pl.ds(slab_base(chain, STATE_TILES + tile), ts), :] for tile in range(STATE_TILES)], axis=1)
        y = (jnp.dot(xr.astype(BF16), cre_ref[ss], preferred_element_type=F32)
             - jnp.dot(xi.astype(BF16), cim_ref[ss], preferred_element_type=F32))
        y = y + d_ref[:, set_cols(ss)] * normed(ss, b)
        o_ref[b, :, set_cols(ss)] = jax.nn.gelu(y).astype(o_ref.dtype)


def _s5_scan(h3, inv3, g, bmat, cre, cim, lam_r, lam_i, d_skip):
    nb, lp, d = h3.shape
    nset = bmat.shape[0]
    per = S5_SETS_PER_STEP
    cols = per * S5_SET_IN
    chains = per * nb
    return pl.pallas_call(
        _s5_scan_body,
        grid=(nset // per, lp // SEQ_BLOCK),
        in_specs=[
            pl.BlockSpec((nb, SEQ_BLOCK, cols), lambda s, c: (0, c, s)),
            pl.BlockSpec((nb, SEQ_BLOCK, LANES), lambda s, c: (0, c, 0)),
            pl.BlockSpec((1, cols), lambda s, c: (0, s)),
            pl.BlockSpec((per, S5_SET_IN, 2 * S5_SET_STATE), lambda s, c: (s, 0, 0)),
            pl.BlockSpec((per, S5_SET_STATE, S5_SET_IN), lambda s, c: (s, 0, 0)),
            pl.BlockSpec((per, S5_SET_STATE, S5_SET_IN), lambda s, c: (s, 0, 0)),
            pl.BlockSpec((per, STATE_TILES, LANES), lambda s, c: (s, 0, 0)),
            pl.BlockSpec((per, STATE_TILES, LANES), lambda s, c: (s, 0, 0)),
            pl.BlockSpec((1, cols), lambda s, c: (0, s)),
        ],
        out_specs=pl.BlockSpec((nb, SEQ_BLOCK, cols), lambda s, c: (0, c, s)),
        out_shape=jax.ShapeDtypeStruct((nb, lp, d), BF16),
        scratch_shapes=[pltpu.VMEM((chains * 2 * STATE_TILES * SLAB_PITCH, LANES), F32),
                        pltpu.VMEM((2 * chains, STATE_TILES, LANES), F32)],
        compiler_params=_params(("parallel", "arbitrary")),
        name="s5_scan",
    )(h3, inv3, g.reshape(1, d), bmat, cre, cim, lam_r, lam_i, d_skip.reshape(1, d))


def _attn_conv_layer(h, g, w_qkv, w_fg, w_gates, w_o, layer, b_f, conv_w, conv_b, batch, lp, pad):
    u = _rmsnorm(h, g[0], BF16)
    qkv = _matmul_slabs(u, w_qkv, layer, 0, w_qkv.shape[2], BF16, 768, "qkv_proj")
    fg = _matmul_slabs(u, w_fg, layer, 0, LANES, F32, LANES, "forget_proj")[0]
    b_pad = jnp.pad(b_f, (0, LANES - ATTN_HEADS)).reshape(1, LANES)
    negc = _neg_cum_logf(fg, b_pad, batch, lp, pad)
    negc = jnp.transpose(negc[:, :ATTN_HEADS].reshape(batch, lp // SEQ_BLOCK, SEQ_BLOCK, HEAD_PAIRS, 2),
                         (0, 3, 1, 4, 2))
    attn = _fox_attn(qkv, negc, batch, lp)
    conv = _gate_conv(u, w_gates, layer, conv_w, conv_b)
    return _out_proj_residual(attn, conv, w_o, layer, h, g[1])


def _s5_layer(h, inv_rms, g, a_re, a_im, log_step, b_re, b_im, c_re, c_im, d_skip, w_glu1, w_glu2, layer, batch, lp):
    m, d = h.shape
    operands = _s5_operands(a_re, a_im, log_step, b_re, b_im, c_re, c_im)
    act = _s5_scan(h.reshape(batch, lp, d), inv_rms.reshape(batch, lp, LANES), g[0], *operands, d_skip)
    return _glu_residual(act.reshape(m, d), w_glu1, w_glu2, layer, h, g[1])


def _projection_weights(w_in, nch=256):
    layers, d, _ = w_in.shape
    qkv_w = 3 * ATTN_WIDTH
    w_qkv = w_in[..., :qkv_w].astype(BF16)
    w_fg = jnp.pad(w_in[..., qkv_w:qkv_w + ATTN_HEADS], ((0, 0), (0, 0), (0, LANES - ATTN_HEADS))).astype(BF16)
    gates = w_in[..., qkv_w + ATTN_HEADS:]
    blocks = gates.shape[2] // (3 * nch)
    gates = jnp.transpose(gates.reshape(layers, d, 3, blocks, nch), (0, 1, 3, 2, 4))
    return w_qkv, w_fg, gates.reshape(layers, d, 3 * blocks * nch).astype(BF16)


def kernel(x, meta_tokens, norm_g, ab_w_in, ab_b_f, ab_conv_w, ab_conv_b, ab_w_o, s5_a_re, s5_a_im, s5_log_step, s5_b_re, s5_b_im, s5_c_re, s5_c_im, s5_d, s5_w_glu1, s5_w_glu2, ffn_w_gate, ffn_w_up, ffn_w_down):
    batch, seq, d = x.shape
    depth = norm_g.shape[0]
    tokens = N_META + seq
    lp = pl.cdiv(tokens, SEQ_BLOCK) * SEQ_BLOCK
    assert (batch * lp) % ROW_BLOCK == 0
    pad = lp - tokens
    meta = jnp.broadcast_to(meta_tokens.astype(x.dtype)[None], (batch, N_META, d))
    h = jnp.concatenate([jnp.zeros((batch, pad, d), x.dtype), meta, x], axis=1).reshape(batch * lp, d)
    w_qkv, w_fg, w_gates = _projection_weights(ab_w_in)
    w_o = ab_w_o.astype(BF16)
    w_glu1 = s5_w_glu1.astype(BF16)
    w_glu2 = s5_w_glu2.astype(BF16)
    w_gate = ffn_w_gate.astype(BF16)
    w_up = ffn_w_up.astype(BF16)
    w_down = ffn_w_down.astype(BF16)
    inv_rms = None
    for i in range(depth):
        g = norm_g[i]
        j = i // 2
        if i % 2 == 0:
            h = _attn_conv_layer(h, g, w_qkv, w_fg, w_gates, w_o, j, ab_b_f[j], ab_conv_w[j], ab_conv_b[j],
                                 batch, lp, pad)
        else:
            h = _s5_layer(h, inv_rms, g, s5_a_re[j], s5_a_im[j], s5_log_step[j], s5_b_re[j], s5_b_im[j],
                          s5_c_re[j], s5_c_im[j], s5_d[j], w_glu1, w_glu2, j, batch, lp)
        next_is_s5 = i + 1 < depth and (i + 1) % 2 == 1
        out = _ffn_residual(h, g[2], g[3], w_gate, w_up, w_down, i, next_is_s5)
        h, inv_rms = out if next_is_s5 else (out, None)
    return h.reshape(batch, lp, d)[:, pad + N_META:]
```

```python
import functools
import math

import jax
import jax.numpy as jnp
from jax import lax
from jax.experimental import pallas as pl
from jax.experimental.pallas import tpu as pltpu

F32 = jnp.float32
BF16 = jnp.bfloat16

N_META = 16
ATTN_HEADS = 16
HEAD_DIM = 64
ATTN_WIDTH = ATTN_HEADS * HEAD_DIM
CONV_K = 3
S5_GROUP = 16
S5_STATE = 64
S5_MIN_DECAY = 1e-4
NORM_EPS = 1e-6

LANES = 128
SUBLANES = 8
ROW_BLOCK = 768
SEQ_BLOCK = 384
HEAD_PAIRS = ATTN_HEADS * HEAD_DIM // LANES
S5_SET_GROUPS = 16
S5_SET_IN = S5_SET_GROUPS * S5_GROUP
S5_SET_STATE = S5_SET_GROUPS * S5_STATE
S5_SETS_PER_STEP = 2
STATE_TILES = S5_SET_STATE // LANES
SLAB_PITCH = SEQ_BLOCK + SUBLANES // 2
KEY_MASK = -1e30
VMEM_LIMIT = 60 * 1024 * 1024


def _params(sem):
    return pltpu.CompilerParams(dimension_semantics=sem, vmem_limit_bytes=VMEM_LIMIT)


def _rmsnorm_body(x_ref, g_ref, o_ref):
    x = x_ref[...]
    ms = jnp.mean(x * x, axis=-1, keepdims=True)
    o_ref[...] = (x * lax.rsqrt(ms + NORM_EPS) * g_ref[...]).astype(o_ref.dtype)


def _rmsnorm(h, g, out_dtype):
    m, d = h.shape
    return pl.pallas_call(
        _rmsnorm_body,
        grid=(m // ROW_BLOCK,),
        in_specs=[pl.BlockSpec((ROW_BLOCK, d), lambda i: (i, 0)),
                  pl.BlockSpec((1, d), lambda i: (0, 0))],
        out_specs=pl.BlockSpec((ROW_BLOCK, d), lambda i: (i, 0)),
        out_shape=jax.ShapeDtypeStruct((m, d), out_dtype),
        compiler_params=_params(("parallel",)),
        name="rmsnorm",
    )(h, g.reshape(1, d))


def _matmul_slabs_body(x_ref, w_ref, o_ref, *, nslab):
    r = jnp.dot(x_ref[...], w_ref[...], preferred_element_type=F32)
    for c in range(nslab):
        o_ref[c] = r[:, c * LANES:(c + 1) * LANES].astype(o_ref.dtype)


def _matmul_slabs(x, w, layer, col0, n, out_dtype, tn, name):
    m, k = x.shape
    nslab = tn // LANES
    assert col0 % tn == 0 and n % tn == 0
    first = col0 // tn
    return pl.pallas_call(
        functools.partial(_matmul_slabs_body, nslab=nslab),
        grid=(m // ROW_BLOCK, n // tn),
        in_specs=[pl.BlockSpec((ROW_BLOCK, k), lambda i, j: (i, 0)),
                  pl.BlockSpec((None, k, tn), lambda i, j: (layer, 0, first + j))],
        out_specs=pl.BlockSpec((nslab, ROW_BLOCK, LANES), lambda i, j: (j, i, 0)),
        out_shape=jax.ShapeDtypeStruct((n // LANES, m, LANES), out_dtype),
        compiler_params=_params(("parallel", "parallel")),
        name=name,
    )(x, w)


def _neg_cum_logf_body(fg_ref, b_ref, o_ref, carry_ref, *, pad):
    c = pl.program_id(1)

    @pl.when(c == 0)
    def _():
        carry_ref[...] = jnp.zeros_like(carry_ref)

    z = fg_ref[...] + b_ref[...]
    logf = jnp.minimum(z, 0.0) - jnp.log1p(jnp.exp(-jnp.abs(z)))
    t = logf.shape[0]
    row = lax.broadcasted_iota(jnp.int32, (t, t), 0)
    col = lax.broadcasted_iota(jnp.int32, (t, t), 1)
    tri = (col <= row).astype(BF16)
    hi = logf.astype(BF16)
    rem = logf - hi.astype(F32)
    mid = rem.astype(BF16)
    lo = (rem - mid.astype(F32)).astype(BF16)
    cs = (jnp.dot(tri, hi, preferred_element_type=F32)
          + jnp.dot(tri, mid, preferred_element_type=F32)
          + jnp.dot(tri, lo, preferred_element_type=F32)) + carry_ref[...]
    carry_ref[...] = cs[t - 1:t, :]
    pos = c * t + lax.broadcasted_iota(jnp.int32, (t, 1), 0)
    o_ref[...] = jnp.where(pos < pad, KEY_MASK, -cs)


def _neg_cum_logf(fg, b_f, batch, lp, pad):
    m = fg.shape[0]
    nblk = lp // SEQ_BLOCK
    return pl.pallas_call(
        functools.partial(_neg_cum_logf_body, pad=pad),
        grid=(batch, nblk),
        in_specs=[pl.BlockSpec((SEQ_BLOCK, LANES), lambda b, c: (b * nblk + c, 0)),
                  pl.BlockSpec((1, LANES), lambda b, c: (0, 0))],
        out_specs=pl.BlockSpec((SEQ_BLOCK, LANES), lambda b, c: (b * nblk + c, 0)),
        out_shape=jax.ShapeDtypeStruct((m, LANES), F32),
        scratch_shapes=[pltpu.VMEM((1, LANES), F32)],
        compiler_params=_params(("parallel", "arbitrary")),
        name="neg_cum_logf",
    )(fg, b_f)


def _fox_attn_body(q_ref, k_ref, v_ref, nc_ref, o_ref, qh_ref, vaug_ref, s_ref, p_ref, scale_ref, m_ref, acc_ref):
    qi = pl.program_id(2)
    tq = q_ref.shape[0]
    nblk = vaug_ref.shape[0]
    lane = lax.broadcasted_iota(jnp.int32, (1, LANES), 1)
    first_head = lane < HEAD_DIM

    @pl.when(qi == 0)
    def _():
        ones_a = jnp.broadcast_to((lane == 0).astype(BF16), (tq, LANES))
        ones_b = jnp.broadcast_to((lane == 1).astype(BF16), (tq, LANES))
        for kb in range(nblk):
            v = v_ref[kb * tq:(kb + 1) * tq, :]
            zero = jnp.zeros_like(v)
            vaug_ref[kb, 0:tq, 0:LANES] = jnp.where(first_head, v, zero)
            vaug_ref[kb, tq:2 * tq, 0:LANES] = jnp.where(first_head, zero, v)
            vaug_ref[kb, 0:tq, LANES:2 * LANES] = ones_a
            vaug_ref[kb, tq:2 * tq, LANES:2 * LANES] = ones_b

    q = q_ref[...]
    zero = jnp.zeros_like(q)
    qh_ref[0:tq, :] = jnp.where(first_head, q, zero) * (HEAD_DIM ** -0.5)
    qh_ref[tq:2 * tq, :] = jnp.where(first_head, zero, q) * (HEAD_DIM ** -0.5)
    m_ref[...] = jnp.full_like(m_ref, -jnp.inf)
    acc_ref[...] = jnp.zeros_like(acc_ref)

    def scores(ki, slot):
        rows = pl.ds(pl.multiple_of(ki * tq, tq), tq)
        s = lax.dot_general(qh_ref[...], k_ref[rows, :], (((1,), (1,)), ((), ())),
                            preferred_element_type=F32)
        nc = nc_ref[ki]
        s_ref[slot, 0:tq, :] = s[0:tq] + nc[0:1, :]
        s_ref[slot, tq:2 * tq, :] = s[tq:2 * tq] + nc[1:2, :]

    def softmax(slot, diagonal):
        s = s_ref[slot]
        if diagonal:
            row = lax.broadcasted_iota(jnp.int32, (2 * tq, tq), 0)
            row = jnp.where(row >= tq, row - tq, row)
            col = lax.broadcasted_iota(jnp.int32, (2 * tq, tq), 1)
            s = jnp.where(col <= row, s, -jnp.inf)
        m_prev = m_ref[...]
        m_next = jnp.maximum(m_prev, jnp.max(s, axis=-1, keepdims=True))
        m_ref[...] = m_next
        alpha = jnp.exp(m_prev - m_next)
        p = jnp.exp(s - jnp.concatenate([m_next] * (tq // LANES), axis=1)).astype(BF16)
        p_ref[slot, :, 0:tq] = p[0:tq]
        p_ref[slot, :, tq:2 * tq] = p[tq:2 * tq]
        alpha_a = alpha[0:tq]
        alpha_b = alpha[tq:2 * tq]
        scale_ref[slot, :, 0:LANES] = jnp.where(first_head, alpha_a, alpha_b)
        scale_ref[slot, :, LANES:2 * LANES] = jnp.where(lane == 0, alpha_a, alpha_b)

    def accumulate(ki, slot):
        pv = jnp.dot(p_ref[slot], vaug_ref[ki], preferred_element_type=F32)
        acc_ref[...] = scale_ref[slot] * acc_ref[...] + pv

    scores(qi, 0)
    softmax(0, True)
    scores(0, 1)

    def visible_pair(j, carry):
        accumulate(jnp.where(j == 0, qi, 2 * j - 1), 0)
        softmax(1, False)
        scores(2 * j + 1, 0)
        accumulate(2 * j, 1)
        softmax(0, False)
        scores(2 * j + 2, 1)
        return carry

    npairs = qi // 2
    lax.fori_loop(0, npairs, visible_pair, 0)
    accumulate(jnp.where(npairs == 0, qi, 2 * npairs - 1), 0)

    @pl.when(lax.rem(qi, 2) == 1)
    def _():
        softmax(1, False)
        accumulate(qi - 1, 1)

    acc = acc_ref[...]
    sum_a = jnp.broadcast_to(acc[:, LANES:LANES + 1], (tq, LANES))
    sum_b = jnp.broadcast_to(acc[:, LANES + 1:LANES + 2], (tq, LANES))
    o_ref[...] = (acc[:, 0:LANES] / jnp.where(first_head, sum_a, sum_b)).astype(o_ref.dtype)


def _fox_attn(qkv, negc, batch, lp):
    nslab, m, _ = qkv.shape
    nq = lp // SEQ_BLOCK
    qkv_seq = qkv.reshape(nslab, batch, lp, LANES)
    seq_blk = (None, None, lp, LANES)
    return pl.pallas_call(
        _fox_attn_body,
        grid=(batch, HEAD_PAIRS, nq),
        in_specs=[
            pl.BlockSpec((None, SEQ_BLOCK, LANES), lambda b, p, qi: (p, b * nq + qi, 0)),
            pl.BlockSpec(seq_blk, lambda b, p, qi: (HEAD_PAIRS + p, b, 0, 0)),
            pl.BlockSpec(seq_blk, lambda b, p, qi: (2 * HEAD_PAIRS + p, b, 0, 0)),
            pl.BlockSpec((None, None, nq, 2, SEQ_BLOCK), lambda b, p, qi: (b, p, 0, 0, 0)),
        ],
        out_specs=pl.BlockSpec((SEQ_BLOCK, LANES), lambda b, p, qi: (b * nq + qi, p)),
        out_shape=jax.ShapeDtypeStruct((m, ATTN_WIDTH), BF16),
        scratch_shapes=[pltpu.VMEM((2 * SEQ_BLOCK, LANES), BF16),
                        pltpu.VMEM((nq, 2 * SEQ_BLOCK, 2 * LANES), BF16),
                        pltpu.VMEM((2, 2 * SEQ_BLOCK, SEQ_BLOCK), F32),
                        pltpu.VMEM((2, SEQ_BLOCK, 2 * SEQ_BLOCK), BF16),
                        pltpu.VMEM((2, SEQ_BLOCK, 2 * LANES), F32),
                        pltpu.VMEM((2 * SEQ_BLOCK, LANES), F32),
                        pltpu.VMEM((SEQ_BLOCK, 2 * LANES), F32)],
        compiler_params=_params(("parallel", "parallel", "arbitrary")),
        name="fox_attn",
    )(qkv, qkv_seq, qkv_seq, negc)


CONV_HALO = 16


def _gate_conv_body(x_ref, xh_ref, wb_ref, wc_ref, wx_ref, cw_ref, cb_ref, o_ref, xcat_ref, z_ref):
    i = pl.program_id(0)
    j = pl.program_id(1)
    tm = x_ref.shape[0]

    @pl.when(j == 0)
    def _():
        xcat_ref[0:CONV_HALO, :] = xh_ref[...]
        xcat_ref[CONV_HALO:CONV_HALO + tm, :] = x_ref[...]

    xcat = xcat_ref[...]
    gate_b = jnp.dot(xcat, wb_ref[...], preferred_element_type=F32)
    gate_c = jnp.dot(xcat, wc_ref[...], preferred_element_type=F32)
    conv_in = jnp.dot(xcat, wx_ref[...], preferred_element_type=F32)
    z_all = gate_c * conv_in
    z_ref[...] = z_all
    head = z_all[0:CONV_HALO]
    z_ref[0:CONV_HALO, :] = jnp.where(i > 0, head, jnp.zeros_like(head))
    z = z_all[CONV_HALO:CONV_HALO + tm]
    z1 = z_ref[CONV_HALO - 1:CONV_HALO - 1 + tm, :]
    z2 = z_ref[CONV_HALO - 2:CONV_HALO - 2 + tm, :]
    conv = cw_ref[0:1, :] * z2 + cw_ref[1:2, :] * z1 + cw_ref[2:3, :] * z + cb_ref[...]
    o_ref[...] = (gate_b[CONV_HALO:CONV_HALO + tm] * conv).astype(o_ref.dtype)


def _gate_conv(x, w, layer, conv_w, conv_b, nch=256):
    m, k = x.shape
    channels = w.shape[2] // 3
    nblk = channels // nch
    halo_per_block = ROW_BLOCK // CONV_HALO

    def w_spec(kind):
        return pl.BlockSpec((None, k, nch), lambda i, j: (layer, 0, kind * nblk + j))

    return pl.pallas_call(
        _gate_conv_body,
        grid=(m // ROW_BLOCK, nblk),
        in_specs=[
            pl.BlockSpec((ROW_BLOCK, k), lambda i, j: (i, 0)),
            pl.BlockSpec((CONV_HALO, k), lambda i, j: (jnp.maximum(i * halo_per_block - 1, 0), 0)),
            w_spec(0), w_spec(1), w_spec(2),
            pl.BlockSpec((CONV_K, nch), lambda i, j: (0, j)),
            pl.BlockSpec((1, nch), lambda i, j: (0, j)),
        ],
        out_specs=pl.BlockSpec((ROW_BLOCK, nch), lambda i, j: (i, j)),
        out_shape=jax.ShapeDtypeStruct((m, channels), BF16),
        scratch_shapes=[pltpu.VMEM((ROW_BLOCK + CONV_HALO, k), BF16),
                        pltpu.VMEM((ROW_BLOCK + CONV_HALO, nch), F32)],
        compiler_params=_params(("parallel", "arbitrary")),
        name="gate_conv",
    )(x, x, w, w, w, conv_w, conv_b.reshape(1, -1))


def _residual_norm_store(m_ref, h_ref, g_ref, o_ref):
    nchunk, _, tn = m_ref.shape
    ss = None
    for c in range(nchunk):
        mc = m_ref[c]
        part = jnp.sum(mc * mc, axis=-1, keepdims=True)
        ss = part if ss is None else ss + part
    scale = lax.rsqrt(ss / (nchunk * tn) + NORM_EPS)
    for c in range(nchunk):
        cols = slice(c * tn, (c + 1) * tn)
        o_ref[:, cols] = h_ref[:, cols] + m_ref[c] * scale * g_ref[:, cols]


def _out_proj_body(xa_ref, xb_ref, w_ref, h_ref, g_ref, o_ref, m_ref):
    nchunk, _, tn = m_ref.shape
    ka = xa_ref.shape[1]
    xa = xa_ref[...]
    xb = xb_ref[...]
    for c in range(nchunk):
        cols = slice(c * tn, (c + 1) * tn)
        m_ref[c] = (jnp.dot(xa, w_ref[0:ka, cols], preferred_element_type=F32)
                    + jnp.dot(xb, w_ref[ka:, cols], preferred_element_type=F32))
    _residual_norm_store(m_ref, h_ref, g_ref, o_ref)


def _out_proj_residual(xa, xb, w, layer, h, g, tn=512):
    m, ka = xa.shape
    kb = xb.shape[1]
    n = w.shape[2]
    return pl.pallas_call(
        _out_proj_body,
        grid=(m // ROW_BLOCK,),
        in_specs=[
            pl.BlockSpec((ROW_BLOCK, ka), lambda i: (i, 0)),
            pl.BlockSpec((ROW_BLOCK, kb), lambda i: (i, 0)),
            pl.BlockSpec((None, ka + kb, n), lambda i: (layer, 0, 0)),
            pl.BlockSpec((ROW_BLOCK, n), lambda i: (i, 0)),
            pl.BlockSpec((1, n), lambda i: (0, 0)),
        ],
        out_specs=pl.BlockSpec((ROW_BLOCK, n), lambda i: (i, 0)),
        out_shape=jax.ShapeDtypeStruct((m, n), F32),
        scratch_shapes=[pltpu.VMEM((n // tn, ROW_BLOCK, tn), F32)],
        compiler_params=_params(("parallel",)),
        name="out_proj_residual",
    )(xa, xb, w, h, g.reshape(1, n))


def _glu_body(x_ref, w1_ref, w2_ref, h_ref, g_ref, o_ref, m_ref):
    j = pl.program_id(1)
    x = x_ref[...]
    a = jnp.dot(x, w1_ref[...], preferred_element_type=F32)
    b = jnp.dot(x, w2_ref[...], preferred_element_type=F32)
    m_ref[j] = a * jax.nn.sigmoid(b)

    @pl.when(j == pl.num_programs(1) - 1)
    def _():
        _residual_norm_store(m_ref, h_ref, g_ref, o_ref)


def _glu_residual(x, w1, w2, layer, h, g, tn=512):
    m, k = x.shape
    n = w1.shape[2]
    return pl.pallas_call(
        _glu_body,
        grid=(m // ROW_BLOCK, n // tn),
        in_specs=[
            pl.BlockSpec((ROW_BLOCK, k), lambda i, j: (i, 0)),
            pl.BlockSpec((None, k, tn), lambda i, j: (layer, 0, j)),
            pl.BlockSpec((None, k, tn), lambda i, j: (layer, 0, j)),
            pl.BlockSpec((ROW_BLOCK, n), lambda i, j: (i, 0)),
            pl.BlockSpec((1, n), lambda i, j: (0, 0)),
        ],
        out_specs=pl.BlockSpec((ROW_BLOCK, n), lambda i, j: (i, 0)),
        out_shape=jax.ShapeDtypeStruct((m, n), F32),
        scratch_shapes=[pltpu.VMEM((n // tn, ROW_BLOCK, tn), F32)],
        compiler_params=_params(("parallel", "arbitrary")),
        name="glu_residual",
    )(x, w1, w2, h, g.reshape(1, n))


def _ffn_body(h_ref, gin_ref, gout_ref, wg_ref, wu_ref, wd_ref, o_ref, *rest, emit_inv_rms):
    inv_ref = rest[0] if emit_inv_rms else None
    hn_ref, acc_ref = rest[-2:]
    j = pl.program_id(1)

    @pl.when(j == 0)
    def _():
        x = h_ref[...]
        ms = jnp.mean(x * x, axis=-1, keepdims=True)
        hn_ref[...] = (x * lax.rsqrt(ms + NORM_EPS) * gin_ref[...]).astype(hn_ref.dtype)
        acc_ref[...] = jnp.zeros_like(acc_ref)

    hn = hn_ref[...]
    gate = jnp.dot(hn, wg_ref[...], preferred_element_type=F32)
    up = jnp.dot(hn, wu_ref[...], preferred_element_type=F32)
    act = (gate * jax.nn.sigmoid(gate) * up).astype(BF16)
    acc_ref[...] += jnp.dot(act, wd_ref[...].astype(BF16), preferred_element_type=F32)

    @pl.when(j == pl.num_programs(1) - 1)
    def _():
        f = acc_ref[...]
        ms = jnp.mean(f * f, axis=-1, keepdims=True)
        out = h_ref[...] + f * lax.rsqrt(ms + NORM_EPS) * gout_ref[...]
        o_ref[...] = out
        if emit_inv_rms:
            ms_out = jnp.mean(out * out, axis=-1, keepdims=True)
            inv_ref[...] = jnp.broadcast_to(lax.rsqrt(ms_out + NORM_EPS), inv_ref.shape)


def _ffn_residual(h, g_in, g_out, w_gate, w_up, w_down, layer, emit_inv_rms, th=512):
    m, d = h.shape
    hidden = w_gate.shape[2]
    row_spec = pl.BlockSpec((ROW_BLOCK, d), lambda i, j: (i, 0))
    out_specs = [row_spec]
    out_shape = [jax.ShapeDtypeStruct((m, d), F32)]
    if emit_inv_rms:
        out_specs.append(pl.BlockSpec((ROW_BLOCK, LANES), lambda i, j: (i, 0)))
        out_shape.append(jax.ShapeDtypeStruct((m, LANES), F32))
    outs = pl.pallas_call(
        functools.partial(_ffn_body, emit_inv_rms=emit_inv_rms),
        grid=(m // ROW_BLOCK, hidden // th),
        in_specs=[
            row_spec,
            pl.BlockSpec((1, d), lambda i, j: (0, 0)),
            pl.BlockSpec((1, d), lambda i, j: (0, 0)),
            pl.BlockSpec((None, d, th), lambda i, j: (layer, 0, j)),
            pl.BlockSpec((None, d, th), lambda i, j: (layer, 0, j)),
            pl.BlockSpec((None, th, d), lambda i, j: (layer, j, 0)),
        ],
        out_specs=out_specs,
        out_shape=out_shape,
        scratch_shapes=[pltpu.VMEM((ROW_BLOCK, d), BF16),
                        pltpu.VMEM((ROW_BLOCK, d), F32)],
        compiler_params=_params(("parallel", "arbitrary")),
        name="ffn_residual",
    )(h, g_in.reshape(1, d), g_out.reshape(1, d), w_gate, w_up, w_down)
    return tuple(outs) if emit_inv_rms else outs[0]


def _lane_repeat(x, reps, exact_f32):
    n = x.shape[1]
    row = lax.broadcasted_iota(jnp.int32, (n, n * reps), 0)
    col = lax.broadcasted_iota(jnp.int32, (n, n * reps), 1)
    sel = (jnp.bitwise_and(col, n - 1) == row).astype(BF16)
    hi = x.astype(BF16)
    out = jnp.dot(hi, sel, preferred_element_type=F32)
    if exact_f32:
        rem = x - hi.astype(F32)
        mid = rem.astype(BF16)
        lo = (rem - mid.astype(F32)).astype(BF16)
        out = out + jnp.dot(mid, sel, preferred_element_type=F32) + jnp.dot(lo, sel, preferred_element_type=F32)
    return out


def _diag_block_mask(shape, row_shift, col_shift):
    row = lax.broadcasted_iota(jnp.int32, shape, 0)
    col = lax.broadcasted_iota(jnp.int32, shape, 1)
    return jnp.right_shift(row, row_shift) == jnp.right_shift(col, col_shift)


def _s5_operands_body(are_ref, aim_ref, ls_ref, bre_ref, bim_ref, cre_ref, cim_ref,
                      lbr_ref, lbi_ref, bmat_ref, cmr_ref, cmi_ref):
    lam_re = jnp.minimum(are_ref[...], -S5_MIN_DECAY)
    lam_im = aim_ref[...]
    delta = jnp.exp(ls_ref[...])
    mag = jnp.exp(lam_re * delta)
    ang = lam_im * delta
    lb_re = mag * jnp.cos(ang)
    lb_im = mag * jnp.sin(ang)
    den = lam_re * lam_re + lam_im * lam_im
    nr = lb_re - 1.0
    ni = lb_im
    coef_re = (nr * lam_re + ni * lam_im) / den
    coef_im = (ni * lam_re - nr * lam_im) / den
    lbr_ref[...] = lb_re
    lbi_ref[...] = lb_im

    br = _lane_repeat(bre_ref[...], S5_SET_GROUPS, True)
    bi = _lane_repeat(bim_ref[...], S5_SET_GROUPS, True)
    in_mask = _diag_block_mask(br.shape, S5_GROUP.bit_length() - 1, S5_STATE.bit_length() - 1)
    zero = jnp.zeros_like(br)
    bmat_ref[:, 0:S5_SET_STATE] = jnp.where(in_mask, coef_re * br - coef_im * bi, zero).astype(bmat_ref.dtype)
    bmat_ref[:, S5_SET_STATE:2 * S5_SET_STATE] = jnp.where(
        in_mask, coef_re * bi + coef_im * br, zero).astype(bmat_ref.dtype)

    cr = _lane_repeat(cre_ref[...], S5_SET_GROUPS, False)
    ci = _lane_repeat(cim_ref[...], S5_SET_GROUPS, False)
    out_mask = _diag_block_mask(cr.shape, S5_STATE.bit_length() - 1, S5_GROUP.bit_length() - 1)
    cmr_ref[...] = jnp.where(out_mask, cr, jnp.zeros_like(cr)).astype(cmr_ref.dtype)
    cmi_ref[...] = jnp.where(out_mask, ci, jnp.zeros_like(ci)).astype(cmi_ref.dtype)


def _s5_operands(a_re, a_im, log_step, b_re, b_im, c_re, c_im):
    groups, state = a_re.shape
    nset = groups // S5_SET_GROUPS
    lane_row = lambda a: a.reshape(nset, 1, S5_SET_STATE)
    step_row = lane_row(jnp.broadcast_to(log_step[:, None], (groups, state)))
    b_rows = lambda b: jnp.transpose(b, (0, 2, 1)).reshape(nset, S5_SET_IN, state)
    c_rows = lambda c: jnp.transpose(c, (0, 2, 1)).reshape(nset, S5_SET_STATE, S5_GROUP)
    row_spec = pl.BlockSpec((None, 1, S5_SET_STATE), lambda s: (s, 0, 0))
    b_spec = pl.BlockSpec((None, S5_SET_IN, state), lambda s: (s, 0, 0))
    c_spec = pl.BlockSpec((None, S5_SET_STATE, S5_GROUP), lambda s: (s, 0, 0))
    cm_spec = pl.BlockSpec((None, S5_SET_STATE, S5_SET_IN), lambda s: (s, 0, 0))
    lb_re, lb_im, bmat, cmr, cmi = pl.pallas_call(
        _s5_operands_body,
        grid=(nset,),
        in_specs=[row_spec, row_spec, row_spec, b_spec, b_spec, c_spec, c_spec],
        out_specs=[row_spec, row_spec,
                   pl.BlockSpec((None, S5_SET_IN, 2 * S5_SET_STATE), lambda s: (s, 0, 0)), cm_spec, cm_spec],
        out_shape=[jax.ShapeDtypeStruct((nset, 1, S5_SET_STATE), F32)] * 2
        + [jax.ShapeDtypeStruct((nset, S5_SET_IN, 2 * S5_SET_STATE), BF16)]
        + [jax.ShapeDtypeStruct((nset, S5_SET_STATE, S5_SET_IN), BF16)] * 2,
        compiler_params=_params(("parallel",)),
        name="s5_operands",
    )(lane_row(a_re), lane_row(a_im), step_row, b_rows(b_re), b_rows(b_im), c_rows(c_re), c_rows(c_im))
    lam_r = lb_re.reshape(nset, STATE_TILES, LANES)
    lam_i = lb_im.reshape(nset, STATE_TILES, LANES)
    return bmat, cmr, cmi, lam_r, lam_i


def _s5_scan_body(h_ref, inv_ref, g_ref, bmat_ref, cre_ref, cim_ref, lr_ref, li_ref, d_ref, o_ref, slab_ref,
                  carry_ref):
    c = pl.program_id(1)
    nb, ts, _ = h_ref.shape
    nsets = bmat_ref.shape[0]
    chains = [(ss, b) for ss in range(nsets) for b in range(nb)]

    @pl.when(c == 0)
    def _():
        carry_ref[...] = jnp.zeros_like(carry_ref)

    def slab_base(chain, tile):
        return (chain * 2 * STATE_TILES + tile) * SLAB_PITCH

    def set_cols(ss):
        return slice(ss * S5_SET_IN, (ss + 1) * S5_SET_IN)

    def normed(ss, b):
        inv = jnp.concatenate([inv_ref[b]] * (S5_SET_IN // LANES), axis=1)
        return h_ref[b, :, set_cols(ss)] * inv * g_ref[:, set_cols(ss)]

    for chain, (ss, b) in enumerate(chains):
        bu = jnp.dot(normed(ss, b).astype(BF16), bmat_ref[ss], preferred_element_type=F32)
        for tile in range(2 * STATE_TILES):
            slab_ref[pl.ds(slab_base(chain, tile), ts), :] = bu[:, tile * LANES:(tile + 1) * LANES]

    lam = [(lr_ref[ss], li_ref[ss]) for ss in range(nsets)]

    def step(t, state):
        new_state = []
        for chain, (ss, b) in enumerate(chains):
            lam_r, lam_i = lam[ss]
            xr, xi = state[2 * chain], state[2 * chain + 1]
            re_rows = pl.ds(slab_base(chain, 0) + t, STATE_TILES, stride=SLAB_PITCH)
            im_rows = pl.ds(slab_base(chain, STATE_TILES) + t, STATE_TILES, stride=SLAB_PITCH)
            nxr = lam_r * xr - lam_i * xi + slab_ref[re_rows, :]
            nxi = lam_r * xi + lam_i * xr + slab_ref[im_rows, :]
            slab_ref[re_rows, :] = nxr
            slab_ref[im_rows, :] = nxi
            new_state += [nxr, nxi]
        return tuple(new_state)

    init = tuple(carry_ref[s] for s in range(2 * len(chains)))
    final = lax.fori_loop(0, ts, step, init, unroll=8)
    for s in range(2 * len(chains)):
        carry_ref[s] = final[s]

    for chain, (ss, b) in enumerate(chains):
        xr = jnp.concatenate(
            [slab_ref[pl.ds(slab_base(chain, tile), ts), :] for tile in range(STATE_TILES)], axis=1)
        xi = jnp.concatenate(
            [slab_ref[pl.ds(slab_base(chain, STATE_TILES + tile), ts), :] for tile in range(STATE_TILES)], axis=1)
        y = (jnp.dot(xr.astype(BF16), cre_ref[ss], preferred_element_type=F32)
             - jnp.dot(xi.astype(BF16), cim_ref[ss], preferred_element_type=F32))
        y = y + d_ref[:, set_cols(ss)] * normed(ss, b)
        o_ref[b, :, set_cols(ss)] = jax.nn.gelu(y).astype(o_ref.dtype)


def _s5_scan(h3, inv3, g, bmat, cre, cim, lam_r, lam_i, d_skip):
    nb, lp, d = h3.shape
    nset = bmat.shape[0]
    per = S5_SETS_PER_STEP
    cols = per * S5_SET_IN
    chains = per * nb
    return pl.pallas_call(
        _s5_scan_body,
        grid=(nset // per, lp // SEQ_BLOCK),
        in_specs=[
            pl.BlockSpec((nb, SEQ_BLOCK, cols), lambda s, c: (0, c, s)),
            pl.BlockSpec((nb, SEQ_BLOCK, LANES), lambda s, c: (0, c, 0)),
            pl.BlockSpec((1, cols), lambda s, c: (0, s)),
            pl.BlockSpec((per, S5_SET_IN, 2 * S5_SET_STATE), lambda s, c: (s, 0, 0)),
            pl.BlockSpec((per, S5_SET_STATE, S5_SET_IN), lambda s, c: (s, 0, 0)),
            pl.BlockSpec((per, S5_SET_STATE, S5_SET_IN), lambda s, c: (s, 0, 0)),
            pl.BlockSpec((per, STATE_TILES, LANES), lambda s, c: (s, 0, 0)),
            pl.BlockSpec((per, STATE_TILES, LANES), lambda s, c: (s, 0, 0)),
            pl.BlockSpec((1, cols), lambda s, c: (0, s)),
        ],
        out_specs=pl.BlockSpec((nb, SEQ_BLOCK, cols), lambda s, c: (0, c, s)),
        out_shape=jax.ShapeDtypeStruct((nb, lp, d), BF16),
        scratch_shapes=[pltpu.VMEM((chains * 2 * STATE_TILES * SLAB_PITCH, LANES), F32),
                        pltpu.VMEM((2 * chains, STATE_TILES, LANES), F32)],
        compiler_params=_params(("parallel", "arbitrary")),
        name="s5_scan",
    )(h3, inv3, g.reshape(1, d), bmat, cre, cim, lam_r, lam_i, d_skip.reshape(1, d))


def _attn_conv_layer(h, g, w_qkv, w_fg, w_gates, w_o, layer, b_f, conv_w, conv_b, batch, lp, pad):
    u = _rmsnorm(h, g[0], BF16)
    qkv = _matmul_slabs(u, w_qkv, layer, 0, w_qkv.shape[2], BF16, 768, "qkv_proj")
    fg = _matmul_slabs(u, w_fg, layer, 0, LANES, F32, LANES, "forget_proj")[0]
    b_pad = jnp.pad(b_f, (0, LANES - ATTN_HEADS)).reshape(1, LANES)
    negc = _neg_cum_logf(fg, b_pad, batch, lp, pad)
    negc = jnp.transpose(negc[:, :ATTN_HEADS].reshape(batch, lp // SEQ_BLOCK, SEQ_BLOCK, HEAD_PAIRS, 2),
                         (0, 3, 1, 4, 2))
    attn = _fox_attn(qkv, negc, batch, lp)
    conv = _gate_conv(u, w_gates, layer, conv_w, conv_b)
    return _out_proj_residual(attn, conv, w_o, layer, h, g[1])


def _s5_layer(h, inv_rms, g, a_re, a_im, log_step, b_re, b_im, c_re, c_im, d_skip, w_glu1, w_glu2, layer, batch, lp):
    m, d = h.shape
    operands = _s5_operands(a_re, a_im, log_step, b_re, b_im, c_re, c_im)
    act = _s5_scan(h.reshape(batch, lp, d), inv_rms.reshape(batch, lp, LANES), g[0], *operands, d_skip)
    return _glu_residual(act.reshape(m, d), w_glu1, w_glu2, layer, h, g[1])


WEIGHT_CAST_ROWS = 256


def _cast_columns_body(*refs, shift):
    o_ref = refs[-1]
    x = refs[0][...]
    if shift:
        x = jnp.concatenate([x, refs[1][...]], axis=1)
        x = pltpu.roll(x, x.shape[1] - shift, axis=1)[:, :o_ref.shape[1]]
    o_ref[...] = x.astype(o_ref.dtype)


def _cast_columns(w, first, width, shift):
    layers, k, _ = w.shape
    assert first % width == 0 and width % LANES == 0 and 0 <= shift < LANES
    in_specs = [pl.BlockSpec((None, WEIGHT_CAST_ROWS, width), lambda l, i: (l, i, first // width))]
    operands = [w]
    if shift:
        in_specs.append(pl.BlockSpec((None, WEIGHT_CAST_ROWS, LANES), lambda l, i: (l, i, (first + width) // LANES)))
        operands.append(w)
    return pl.pallas_call(
        functools.partial(_cast_columns_body, shift=shift),
        grid=(layers, k // WEIGHT_CAST_ROWS),
        in_specs=in_specs,
        out_specs=pl.BlockSpec((None, WEIGHT_CAST_ROWS, width), lambda l, i: (l, i, 0)),
        out_shape=jax.ShapeDtypeStruct((layers, k, width), BF16),
        compiler_params=_params(("parallel", "parallel")),
        name="cast_columns",
    )(*operands)


def _projection_weights(w_in):
    qkv_w = 3 * ATTN_WIDTH
    w_qkv = _cast_columns(w_in, 0, qkv_w, 0)
    w_gates = _cast_columns(w_in, qkv_w, qkv_w, ATTN_HEADS)
    w_fg = jnp.pad(w_in[..., qkv_w:qkv_w + ATTN_HEADS], ((0, 0), (0, 0), (0, LANES - ATTN_HEADS))).astype(BF16)
    return w_qkv, w_fg, w_gates


def kernel(x, meta_tokens, norm_g, ab_w_in, ab_b_f, ab_conv_w, ab_conv_b, ab_w_o, s5_a_re, s5_a_im, s5_log_step, s5_b_re, s5_b_im, s5_c_re, s5_c_im, s5_d, s5_w_glu1, s5_w_glu2, ffn_w_gate, ffn_w_up, ffn_w_down):
    batch, seq, d = x.shape
    depth = norm_g.shape[0]
    tokens = N_META + seq
    lp = pl.cdiv(tokens, SEQ_BLOCK) * SEQ_BLOCK
    assert (batch * lp) % ROW_BLOCK == 0
    pad = lp - tokens
    meta = jnp.broadcast_to(meta_tokens.astype(x.dtype)[None], (batch, N_META, d))
    h = jnp.concatenate([jnp.zeros((batch, pad, d), x.dtype), meta, x], axis=1).reshape(batch * lp, d)
    w_qkv, w_fg, w_gates = _projection_weights(ab_w_in)
    w_o = ab_w_o.astype(BF16)
    w_glu1 = s5_w_glu1.astype(BF16)
    w_glu2 = s5_w_glu2.astype(BF16)
    w_gate = ffn_w_gate.astype(BF16)
    w_up = ffn_w_up.astype(BF16)
    w_down = ffn_w_down
    inv_rms = None
    for i in range(depth):
        g = norm_g[i]
        j = i // 2
        if i % 2 == 0:
            h = _attn_conv_layer(h, g, w_qkv, w_fg, w_gates, w_o, j, ab_b_f[j], ab_conv_w[j], ab_conv_b[j],
                                 batch, lp, pad)
        else:
            h = _s5_layer(h, inv_rms, g, s5_a_re[j], s5_a_im[j], s5_log_step[j], s5_b_re[j], s5_b_im[j],
                          s5_c_re[j], s5_c_im[j], s5_d[j], w_glu1, w_glu2, j, batch, lp)
        next_is_s5 = i + 1 < depth and (i + 1) % 2 == 1
        out = _ffn_residual(h, g[2], g[3], w_gate, w_up, w_down, i, next_is_s5)
        h, inv_rms = out if next_is_s5 else (out, None)
    return h.reshape(batch, lp, d)[:, pad + N_META:]
```

```python
import functools
import math

import jax
import jax.numpy as jnp
from jax import lax
from jax.experimental import pallas as pl
from jax.experimental.pallas import tpu as pltpu

F32 = jnp.float32
BF16 = jnp.bfloat16

N_META = 16
ATTN_HEADS = 16
HEAD_DIM = 64
ATTN_WIDTH = ATTN_HEADS * HEAD_DIM
CONV_K = 3
S5_GROUP = 16
S5_STATE = 64
S5_MIN_DECAY = 1e-4
NORM_EPS = 1e-6

LANES = 128
SUBLANES = 8
ROW_BLOCK = 768
SEQ_BLOCK = 384
HEAD_PAIRS = ATTN_HEADS * HEAD_DIM // LANES
S5_SET_GROUPS = 16
S5_SET_IN = S5_SET_GROUPS * S5_GROUP
S5_SET_STATE = S5_SET_GROUPS * S5_STATE
S5_SETS_PER_STEP = 2
STATE_TILES = S5_SET_STATE // LANES
SLAB_PITCH = SEQ_BLOCK + SUBLANES // 2
KEY_MASK = -1e30
VMEM_LIMIT = 60 * 1024 * 1024


def _params(sem):
    return pltpu.CompilerParams(dimension_semantics=sem, vmem_limit_bytes=VMEM_LIMIT)


def _rmsnorm_body(x_ref, g_ref, o_ref):
    x = x_ref[...]
    ms = jnp.mean(x * x, axis=-1, keepdims=True)
    o_ref[...] = (x * lax.rsqrt(ms + NORM_EPS) * g_ref[...]).astype(o_ref.dtype)


def _rmsnorm(h, g, out_dtype):
    m, d = h.shape
    return pl.pallas_call(
        _rmsnorm_body,
        grid=(m // ROW_BLOCK,),
        in_specs=[pl.BlockSpec((ROW_BLOCK, d), lambda i: (i, 0)),
                  pl.BlockSpec((1, d), lambda i: (0, 0))],
        out_specs=pl.BlockSpec((ROW_BLOCK, d), lambda i: (i, 0)),
        out_shape=jax.ShapeDtypeStruct((m, d), out_dtype),
        compiler_params=_params(("parallel",)),
        name="rmsnorm",
    )(h, g.reshape(1, d))


def _matmul_slabs_body(x_ref, w_ref, o_ref, *, nslab):
    r = jnp.dot(x_ref[...], w_ref[...], preferred_element_type=F32)
    for c in range(nslab):
        o_ref[c] = r[:, c * LANES:(c + 1) * LANES].astype(o_ref.dtype)


def _matmul_slabs(x, w, layer, col0, n, out_dtype, tn, name):
    m, k = x.shape
    nslab = tn // LANES
    assert col0 % tn == 0 and n % tn == 0
    first = col0 // tn
    return pl.pallas_call(
        functools.partial(_matmul_slabs_body, nslab=nslab),
        grid=(m // ROW_BLOCK, n // tn),
        in_specs=[pl.BlockSpec((ROW_BLOCK, k), lambda i, j: (i, 0)),
                  pl.BlockSpec((None, k, tn), lambda i, j: (layer, 0, first + j))],
        out_specs=pl.BlockSpec((nslab, ROW_BLOCK, LANES), lambda i, j: (j, i, 0)),
        out_shape=jax.ShapeDtypeStruct((n // LANES, m, LANES), out_dtype),
        compiler_params=_params(("parallel", "parallel")),
        name=name,
    )(x, w)


def _neg_cum_logf_body(fg_ref, b_ref, o_ref, carry_ref, *, pad):
    c = pl.program_id(1)

    @pl.when(c == 0)
    def _():
        carry_ref[...] = jnp.zeros_like(carry_ref)

    z = fg_ref[...] + b_ref[...]
    logf = jnp.minimum(z, 0.0) - jnp.log1p(jnp.exp(-jnp.abs(z)))
    t = logf.shape[0]
    row = lax.broadcasted_iota(jnp.int32, (t, t), 0)
    col = lax.broadcasted_iota(jnp.int32, (t, t), 1)
    tri = (col <= row).astype(BF16)
    hi = logf.astype(BF16)
    rem = logf - hi.astype(F32)
    mid = rem.astype(BF16)
    lo = (rem - mid.astype(F32)).astype(BF16)
    cs = (jnp.dot(tri, hi, preferred_element_type=F32)
          + jnp.dot(tri, mid, preferred_element_type=F32)
          + jnp.dot(tri, lo, preferred_element_type=F32)) + carry_ref[...]
    carry_ref[...] = cs[t - 1:t, :]
    pos = c * t + lax.broadcasted_iota(jnp.int32, (t, 1), 0)
    o_ref[...] = jnp.where(pos < pad, KEY_MASK, -cs)


def _neg_cum_logf(fg, b_f, batch, lp, pad):
    m = fg.shape[0]
    nblk = lp // SEQ_BLOCK
    return pl.pallas_call(
        functools.partial(_neg_cum_logf_body, pad=pad),
        grid=(batch, nblk),
        in_specs=[pl.BlockSpec((SEQ_BLOCK, LANES), lambda b, c: (b * nblk + c, 0)),
                  pl.BlockSpec((1, LANES), lambda b, c: (0, 0))],
        out_specs=pl.BlockSpec((SEQ_BLOCK, LANES), lambda b, c: (b * nblk + c, 0)),
        out_shape=jax.ShapeDtypeStruct((m, LANES), F32),
        scratch_shapes=[pltpu.VMEM((1, LANES), F32)],
        compiler_params=_params(("parallel", "arbitrary")),
        name="neg_cum_logf",
    )(fg, b_f)


def _fox_attn_body(q_ref, k_ref, v_ref, nc_ref, o_ref, qh_ref, vaug_ref, s_ref, p_ref, scale_ref, m_ref, acc_ref):
    qi = pl.program_id(2)
    tq = q_ref.shape[0]
    nblk = vaug_ref.shape[0]
    lane = lax.broadcasted_iota(jnp.int32, (1, LANES), 1)
    first_head = lane < HEAD_DIM

    @pl.when(qi == 0)
    def _():
        ones_a = jnp.broadcast_to((lane == 0).astype(BF16), (tq, LANES))
        ones_b = jnp.broadcast_to((lane == 1).astype(BF16), (tq, LANES))
        for kb in range(nblk):
            v = v_ref[kb * tq:(kb + 1) * tq, :]
            zero = jnp.zeros_like(v)
            vaug_ref[kb, 0:tq, 0:LANES] = jnp.where(first_head, v, zero)
            vaug_ref[kb, tq:2 * tq, 0:LANES] = jnp.where(first_head, zero, v)
            vaug_ref[kb, 0:tq, LANES:2 * LANES] = ones_a
            vaug_ref[kb, tq:2 * tq, LANES:2 * LANES] = ones_b

    q = q_ref[...]
    zero = jnp.zeros_like(q)
    qh_ref[0:tq, :] = jnp.where(first_head, q, zero) * (HEAD_DIM ** -0.5)
    qh_ref[tq:2 * tq, :] = jnp.where(first_head, zero, q) * (HEAD_DIM ** -0.5)
    m_ref[...] = jnp.full_like(m_ref, -jnp.inf)
    acc_ref[...] = jnp.zeros_like(acc_ref)

    def scores(ki, slot):
        rows = pl.ds(pl.multiple_of(ki * tq, tq), tq)
        s = lax.dot_general(qh_ref[...], k_ref[rows, :], (((1,), (1,)), ((), ())),
                            preferred_element_type=F32)
        nc = nc_ref[ki]
        s_ref[slot, 0:tq, :] = s[0:tq] + nc[0:1, :]
        s_ref[slot, tq:2 * tq, :] = s[tq:2 * tq] + nc[1:2, :]

    def softmax(slot, diagonal):
        s = s_ref[slot]
        if diagonal:
            row = lax.broadcasted_iota(jnp.int32, (2 * tq, tq), 0)
            row = jnp.where(row >= tq, row - tq, row)
            col = lax.broadcasted_iota(jnp.int32, (2 * tq, tq), 1)
            s = jnp.where(col <= row, s, -jnp.inf)
        m_prev = m_ref[...]
        m_next = jnp.maximum(m_prev, jnp.max(s, axis=-1, keepdims=True))
        m_ref[...] = m_next
        alpha = jnp.exp(m_prev - m_next)
        p = jnp.exp(s - jnp.concatenate([m_next] * (tq // LANES), axis=1)).astype(BF16)
        p_ref[slot, :, 0:tq] = p[0:tq]
        p_ref[slot, :, tq:2 * tq] = p[tq:2 * tq]
        alpha_a = alpha[0:tq]
        alpha_b = alpha[tq:2 * tq]
        scale_ref[slot, :, 0:LANES] = jnp.where(first_head, alpha_a, alpha_b)
        scale_ref[slot, :, LANES:2 * LANES] = jnp.where(lane == 0, alpha_a, alpha_b)

    def accumulate(ki, slot):
        pv = jnp.dot(p_ref[slot], vaug_ref[ki], preferred_element_type=F32)
        acc_ref[...] = scale_ref[slot] * acc_ref[...] + pv

    scores(qi, 0)
    softmax(0, True)
    scores(0, 1)

    def visible_pair(j, carry):
        accumulate(jnp.where(j == 0, qi, 2 * j - 1), 0)
        softmax(1, False)
        scores(2 * j + 1, 0)
        accumulate(2 * j, 1)
        softmax(0, False)
        scores(2 * j + 2, 1)
        return carry

    npairs = qi // 2
    lax.fori_loop(0, npairs, visible_pair, 0)
    accumulate(jnp.where(npairs == 0, qi, 2 * npairs - 1), 0)

    @pl.when(lax.rem(qi, 2) == 1)
    def _():
        softmax(1, False)
        accumulate(qi - 1, 1)

    acc = acc_ref[...]
    sum_a = jnp.broadcast_to(acc[:, LANES:LANES + 1], (tq, LANES))
    sum_b = jnp.broadcast_to(acc[:, LANES + 1:LANES + 2], (tq, LANES))
    o_ref[...] = (acc[:, 0:LANES] / jnp.where(first_head, sum_a, sum_b)).astype(o_ref.dtype)


def _fox_attn(qkv, negc, batch, lp):
    nslab, m, _ = qkv.shape
    nq = lp // SEQ_BLOCK
    qkv_seq = qkv.reshape(nslab, batch, lp, LANES)
    seq_blk = (None, None, lp, LANES)
    return pl.pallas_call(
        _fox_attn_body,
        grid=(batch, HEAD_PAIRS, nq),
        in_specs=[
            pl.BlockSpec((None, SEQ_BLOCK, LANES), lambda b, p, qi: (p, b * nq + qi, 0)),
            pl.BlockSpec(seq_blk, lambda b, p, qi: (HEAD_PAIRS + p, b, 0, 0)),
            pl.BlockSpec(seq_blk, lambda b, p, qi: (2 * HEAD_PAIRS + p, b, 0, 0)),
            pl.BlockSpec((None, None, nq, 2, SEQ_BLOCK), lambda b, p, qi: (b, p, 0, 0, 0)),
        ],
        out_specs=pl.BlockSpec((SEQ_BLOCK, LANES), lambda b, p, qi: (b * nq + qi, p)),
        out_shape=jax.ShapeDtypeStruct((m, ATTN_WIDTH), BF16),
        scratch_shapes=[pltpu.VMEM((2 * SEQ_BLOCK, LANES), BF16),
                        pltpu.VMEM((nq, 2 * SEQ_BLOCK, 2 * LANES), BF16),
                        pltpu.VMEM((2, 2 * SEQ_BLOCK, SEQ_BLOCK), F32),
                        pltpu.VMEM((2, SEQ_BLOCK, 2 * SEQ_BLOCK), BF16),
                        pltpu.VMEM((2, SEQ_BLOCK, 2 * LANES), F32),
                        pltpu.VMEM((2 * SEQ_BLOCK, LANES), F32),
                        pltpu.VMEM((SEQ_BLOCK, 2 * LANES), F32)],
        compiler_params=_params(("parallel", "parallel", "arbitrary")),
        name="fox_attn",
    )(qkv, qkv_seq, qkv_seq, negc)


CONV_HALO = 16


def _gate_conv_body(x_ref, xh_ref, wb_ref, wc_ref, wx_ref, cw_ref, cb_ref, o_ref, xcat_ref, z_ref):
    i = pl.program_id(0)
    j = pl.program_id(1)
    tm = x_ref.shape[0]

    @pl.when(j == 0)
    def _():
        xcat_ref[0:CONV_HALO, :] = xh_ref[...]
        xcat_ref[CONV_HALO:CONV_HALO + tm, :] = x_ref[...]

    xcat = xcat_ref[...]
    gate_b = jnp.dot(xcat, wb_ref[...], preferred_element_type=F32)
    gate_c = jnp.dot(xcat, wc_ref[...], preferred_element_type=F32)
    conv_in = jnp.dot(xcat, wx_ref[...], preferred_element_type=F32)
    z_all = gate_c * conv_in
    z_ref[...] = z_all
    head = z_all[0:CONV_HALO]
    z_ref[0:CONV_HALO, :] = jnp.where(i > 0, head, jnp.zeros_like(head))
    z = z_all[CONV_HALO:CONV_HALO + tm]
    z1 = z_ref[CONV_HALO - 1:CONV_HALO - 1 + tm, :]
    z2 = z_ref[CONV_HALO - 2:CONV_HALO - 2 + tm, :]
    conv = cw_ref[0:1, :] * z2 + cw_ref[1:2, :] * z1 + cw_ref[2:3, :] * z + cb_ref[...]
    o_ref[...] = (gate_b[CONV_HALO:CONV_HALO + tm] * conv).astype(o_ref.dtype)


def _gate_conv(x, w, layer, conv_w, conv_b, nch=256):
    m, k = x.shape
    channels = w.shape[2] // 3
    nblk = channels // nch
    halo_per_block = ROW_BLOCK // CONV_HALO

    def w_spec(kind):
        return pl.BlockSpec((None, k, nch), lambda i, j: (layer, 0, kind * nblk + j))

    return pl.pallas_call(
        _gate_conv_body,
        grid=(m // ROW_BLOCK, nblk),
        in_specs=[
            pl.BlockSpec((ROW_BLOCK, k), lambda i, j: (i, 0)),
            pl.BlockSpec((CONV_HALO, k), lambda i, j: (jnp.maximum(i * halo_per_block - 1, 0), 0)),
            w_spec(0), w_spec(1), w_spec(2),
            pl.BlockSpec((CONV_K, nch), lambda i, j: (0, j)),
            pl.BlockSpec((1, nch), lambda i, j: (0, j)),
        ],
        out_specs=pl.BlockSpec((ROW_BLOCK, nch), lambda i, j: (i, j)),
        out_shape=jax.ShapeDtypeStruct((m, channels), BF16),
        scratch_shapes=[pltpu.VMEM((ROW_BLOCK + CONV_HALO, k), BF16),
                        pltpu.VMEM((ROW_BLOCK + CONV_HALO, nch), F32)],
        compiler_params=_params(("parallel", "arbitrary")),
        name="gate_conv",
    )(x, x, w, w, w, conv_w, conv_b.reshape(1, -1))


def _residual_norm_store(m_ref, h_ref, g_ref, o_ref):
    nchunk, _, tn = m_ref.shape
    ss = None
    for c in range(nchunk):
        mc = m_ref[c]
        part = jnp.sum(mc * mc, axis=-1, keepdims=True)
        ss = part if ss is None else ss + part
    scale = lax.rsqrt(ss / (nchunk * tn) + NORM_EPS)
    for c in range(nchunk):
        cols = slice(c * tn, (c + 1) * tn)
        o_ref[:, cols] = h_ref[:, cols] + m_ref[c] * scale * g_ref[:, cols]


def _out_proj_body(xa_ref, xb_ref, w_ref, h_ref, g_ref, o_ref, m_ref):
    nchunk, _, tn = m_ref.shape
    ka = xa_ref.shape[1]
    xa = xa_ref[...]
    xb = xb_ref[...]
    for c in range(nchunk):
        cols = slice(c * tn, (c + 1) * tn)
        m_ref[c] = (jnp.dot(xa, w_ref[0:ka, cols], preferred_element_type=F32)
                    + jnp.dot(xb, w_ref[ka:, cols], preferred_element_type=F32))
    _residual_norm_store(m_ref, h_ref, g_ref, o_ref)


def _out_proj_residual(xa, xb, w, layer, h, g, tn=512):
    m, ka = xa.shape
    kb = xb.shape[1]
    n = w.shape[2]
    return pl.pallas_call(
        _out_proj_body,
        grid=(m // ROW_BLOCK,),
        in_specs=[
            pl.BlockSpec((ROW_BLOCK, ka), lambda i: (i, 0)),
            pl.BlockSpec((ROW_BLOCK, kb), lambda i: (i, 0)),
            pl.BlockSpec((None, ka + kb, n), lambda i: (layer, 0, 0)),
            pl.BlockSpec((ROW_BLOCK, n), lambda i: (i, 0)),
            pl.BlockSpec((1, n), lambda i: (0, 0)),
        ],
        out_specs=pl.BlockSpec((ROW_BLOCK, n), lambda i: (i, 0)),
        out_shape=jax.ShapeDtypeStruct((m, n), F32),
        scratch_shapes=[pltpu.VMEM((n // tn, ROW_BLOCK, tn), F32)],
        compiler_params=_params(("parallel",)),
        name="out_proj_residual",
    )(xa, xb, w, h, g.reshape(1, n))


def _glu_body(x_ref, w1_ref, w2_ref, h_ref, g_ref, o_ref, m_ref):
    nchunk, _, tn = m_ref.shape
    x = x_ref[...]
    for c in range(nchunk):
        cols = slice(c * tn, (c + 1) * tn)
        a = jnp.dot(x, w1_ref[:, cols], preferred_element_type=F32)
        b = jnp.dot(x, w2_ref[:, cols], preferred_element_type=F32)
        m_ref[c] = a * jax.nn.sigmoid(b)
    _residual_norm_store(m_ref, h_ref, g_ref, o_ref)


def _glu_residual(x, w1, w2, layer, h, g, tn=512):
    m, k = x.shape
    n = w1.shape[2]
    w_spec = pl.BlockSpec((None, k, n), lambda i: (layer, 0, 0), pipeline_mode=pl.Buffered(1))
    return pl.pallas_call(
        _glu_body,
        grid=(m // ROW_BLOCK,),
        in_specs=[
            pl.BlockSpec((ROW_BLOCK, k), lambda i: (i, 0)),
            w_spec, w_spec,
            pl.BlockSpec((ROW_BLOCK, n), lambda i: (i, 0)),
            pl.BlockSpec((1, n), lambda i: (0, 0)),
        ],
        out_specs=pl.BlockSpec((ROW_BLOCK, n), lambda i: (i, 0)),
        out_shape=jax.ShapeDtypeStruct((m, n), F32),
        scratch_shapes=[pltpu.VMEM((n // tn, ROW_BLOCK, tn), F32)],
        compiler_params=_params(("parallel",)),
        name="glu_residual",
    )(x, w1, w2, h, g.reshape(1, n))


def _ffn_body(h_ref, gin_ref, gout_ref, wg_ref, wu_ref, wd_ref, o_ref, *rest, emit_inv_rms):
    inv_ref = rest[0] if emit_inv_rms else None
    hn_ref, acc_ref = rest[-2:]
    j = pl.program_id(1)

    @pl.when(j == 0)
    def _():
        x = h_ref[...]
        ms = jnp.mean(x * x, axis=-1, keepdims=True)
        hn_ref[...] = (x * lax.rsqrt(ms + NORM_EPS) * gin_ref[...]).astype(hn_ref.dtype)
        acc_ref[...] = jnp.zeros_like(acc_ref)

    hn = hn_ref[...]
    gate = jnp.dot(hn, wg_ref[...], preferred_element_type=F32)
    up = jnp.dot(hn, wu_ref[...], preferred_element_type=F32)
    act = (gate * jax.nn.sigmoid(gate) * up).astype(BF16)
    acc_ref[...] += jnp.dot(act, wd_ref[...].astype(BF16), preferred_element_type=F32)

    @pl.when(j == pl.num_programs(1) - 1)
    def _():
        f = acc_ref[...]
        ms = jnp.mean(f * f, axis=-1, keepdims=True)
        out = h_ref[...] + f * lax.rsqrt(ms + NORM_EPS) * gout_ref[...]
        o_ref[...] = out
        if emit_inv_rms:
            ms_out = jnp.mean(out * out, axis=-1, keepdims=True)
            inv_ref[...] = jnp.broadcast_to(lax.rsqrt(ms_out + NORM_EPS), inv_ref.shape)


def _ffn_residual(h, g_in, g_out, w_gate, w_up, w_down, layer, emit_inv_rms, th=512):
    m, d = h.shape
    hidden = w_gate.shape[2]
    row_spec = pl.BlockSpec((ROW_BLOCK, d), lambda i, j: (i, 0))
    out_specs = [row_spec]
    out_shape = [jax.ShapeDtypeStruct((m, d), F32)]
    if emit_inv_rms:
        out_specs.append(pl.BlockSpec((ROW_BLOCK, LANES), lambda i, j: (i, 0)))
        out_shape.append(jax.ShapeDtypeStruct((m, LANES), F32))
    outs = pl.pallas_call(
        functools.partial(_ffn_body, emit_inv_rms=emit_inv_rms),
        grid=(m // ROW_BLOCK, hidden // th),
        in_specs=[
            row_spec,
            pl.BlockSpec((1, d), lambda i, j: (0, 0)),
            pl.BlockSpec((1, d), lambda i, j: (0, 0)),
            pl.BlockSpec((None, d, th), lambda i, j: (layer, 0, j)),
            pl.BlockSpec((None, d, th), lambda i, j: (layer, 0, j)),
            pl.BlockSpec((None, th, d), lambda i, j: (layer, j, 0)),
        ],
        out_specs=out_specs,
        out_shape=out_shape,
        scratch_shapes=[pltpu.VMEM((ROW_BLOCK, d), BF16),
                        pltpu.VMEM((ROW_BLOCK, d), F32)],
        compiler_params=_params(("parallel", "arbitrary")),
        name="ffn_residual",
    )(h, g_in.reshape(1, d), g_out.reshape(1, d), w_gate, w_up, w_down)
    return tuple(outs) if emit_inv_rms else outs[0]


def _lane_repeat(x, reps, exact_f32):
    n = x.shape[1]
    row = lax.broadcasted_iota(jnp.int32, (n, n * reps), 0)
    col = lax.broadcasted_iota(jnp.int32, (n, n * reps), 1)
    sel = (jnp.bitwise_and(col, n - 1) == row).astype(BF16)
    hi = x.astype(BF16)
    out = jnp.dot(hi, sel, preferred_element_type=F32)
    if exact_f32:
        rem = x - hi.astype(F32)
        mid = rem.astype(BF16)
        lo = (rem - mid.astype(F32)).astype(BF16)
        out = out + jnp.dot(mid, sel, preferred_element_type=F32) + jnp.dot(lo, sel, preferred_element_type=F32)
    return out


def _diag_block_mask(shape, row_shift, col_shift):
    row = lax.broadcasted_iota(jnp.int32, shape, 0)
    col = lax.broadcasted_iota(jnp.int32, shape, 1)
    return jnp.right_shift(row, row_shift) == jnp.right_shift(col, col_shift)


def _s5_operands_body(are_ref, aim_ref, ls_ref, bre_ref, bim_ref, cre_ref, cim_ref,
                      lbr_ref, lbi_ref, bmat_ref, cmr_ref, cmi_ref):
    lam_re = jnp.minimum(are_ref[...], -S5_MIN_DECAY)
    lam_im = aim_ref[...]
    delta = jnp.exp(ls_ref[...])
    mag = jnp.exp(lam_re * delta)
    ang = lam_im * delta
    lb_re = mag * jnp.cos(ang)
    lb_im = mag * jnp.sin(ang)
    den = lam_re * lam_re + lam_im * lam_im
    nr = lb_re - 1.0
    ni = lb_im
    coef_re = (nr * lam_re + ni * lam_im) / den
    coef_im = (ni * lam_re - nr * lam_im) / den
    lbr_ref[...] = lb_re
    lbi_ref[...] = lb_im

    br = _lane_repeat(bre_ref[...], S5_SET_GROUPS, True)
    bi = _lane_repeat(bim_ref[...], S5_SET_GROUPS, True)
    in_mask = _diag_block_mask(br.shape, S5_GROUP.bit_length() - 1, S5_STATE.bit_length() - 1)
    zero = jnp.zeros_like(br)
    bmat_ref[:, 0:S5_SET_STATE] = jnp.where(in_mask, coef_re * br - coef_im * bi, zero).astype(bmat_ref.dtype)
    bmat_ref[:, S5_SET_STATE:2 * S5_SET_STATE] = jnp.where(
        in_mask, coef_re * bi + coef_im * br, zero).astype(bmat_ref.dtype)

    cr = _lane_repeat(cre_ref[...], S5_SET_GROUPS, False)
    ci = _lane_repeat(cim_ref[...], S5_SET_GROUPS, False)
    out_mask = _diag_block_mask(cr.shape, S5_STATE.bit_length() - 1, S5_GROUP.bit_length() - 1)
    cmr_ref[...] = jnp.where(out_mask, cr, jnp.zeros_like(cr)).astype(cmr_ref.dtype)
    cmi_ref[...] = jnp.where(out_mask, ci, jnp.zeros_like(ci)).astype(cmi_ref.dtype)


def _s5_operands(a_re, a_im, log_step, b_re, b_im, c_re, c_im):
    groups, state = a_re.shape
    nset = groups // S5_SET_GROUPS
    lane_row = lambda a: a.reshape(nset, 1, S5_SET_STATE)
    step_row = lane_row(jnp.broadcast_to(log_step[:, None], (groups, state)))
    b_rows = lambda b: jnp.transpose(b, (0, 2, 1)).reshape(nset, S5_SET_IN, state)
    c_rows = lambda c: jnp.transpose(c, (0, 2, 1)).reshape(nset, S5_SET_STATE, S5_GROUP)
    row_spec = pl.BlockSpec((None, 1, S5_SET_STATE), lambda s: (s, 0, 0))
    b_spec = pl.BlockSpec((None, S5_SET_IN, state), lambda s: (s, 0, 0))
    c_spec = pl.BlockSpec((None, S5_SET_STATE, S5_GROUP), lambda s: (s, 0, 0))
    cm_spec = pl.BlockSpec((None, S5_SET_STATE, S5_SET_IN), lambda s: (s, 0, 0))
    lb_re, lb_im, bmat, cmr, cmi = pl.pallas_call(
        _s5_operands_body,
        grid=(nset,),
        in_specs=[row_spec, row_spec, row_spec, b_spec, b_spec, c_spec, c_spec],
        out_specs=[row_spec, row_spec,
                   pl.BlockSpec((None, S5_SET_IN, 2 * S5_SET_STATE), lambda s: (s, 0, 0)), cm_spec, cm_spec],
        out_shape=[jax.ShapeDtypeStruct((nset, 1, S5_SET_STATE), F32)] * 2
        + [jax.ShapeDtypeStruct((nset, S5_SET_IN, 2 * S5_SET_STATE), BF16)]
        + [jax.ShapeDtypeStruct((nset, S5_SET_STATE, S5_SET_IN), BF16)] * 2,
        compiler_params=_params(("parallel",)),
        name="s5_operands",
    )(lane_row(a_re), lane_row(a_im), step_row, b_rows(b_re), b_rows(b_im), c_rows(c_re), c_rows(c_im))
    lam_r = lb_re.reshape(nset, STATE_TILES, LANES)
    lam_i = lb_im.reshape(nset, STATE_TILES, LANES)
    return bmat, cmr, cmi, lam_r, lam_i


def _s5_scan_body(h_ref, inv_ref, g_ref, bmat_ref, cre_ref, cim_ref, lr_ref, li_ref, d_ref, o_ref, slab_ref,
                  carry_ref):
    c = pl.program_id(1)
    nb, ts, _ = h_ref.shape
    nsets = bmat_ref.shape[0]
    chains = [(ss, b) for ss in range(nsets) for b in range(nb)]

    @pl.when(c == 0)
    def _():
        carry_ref[...] = jnp.zeros_like(carry_ref)

    def slab_base(chain, tile):
        return (chain * 2 * STATE_TILES + tile) * SLAB_PITCH

    def set_cols(ss):
        return slice(ss * S5_SET_IN, (ss + 1) * S5_SET_IN)

    def normed(ss, b):
        inv = jnp.concatenate([inv_ref[b]] * (S5_SET_IN // LANES), axis=1)
        return h_ref[b, :, set_cols(ss)] * inv * g_ref[:, set_cols(ss)]

    for chain, (ss, b) in enumerate(chains):
        bu = jnp.dot(normed(ss, b).astype(BF16), bmat_ref[ss], preferred_element_type=F32)
        for tile in range(2 * STATE_TILES):
            slab_ref[pl.ds(slab_base(chain, tile), ts), :] = bu[:, tile * LANES:(tile + 1) * LANES]

    lam = [(lr_ref[ss], li_ref[ss]) for ss in range(nsets)]

    def step(t, state):
        new_state = []
        for chain, (ss, b) in enumerate(chains):
            lam_r, lam_i = lam[ss]
            xr, xi = state[2 * chain], state[2 * chain + 1]
            re_rows = pl.ds(slab_base(chain, 0) + t, STATE_TILES, stride=SLAB_PITCH)
            im_rows = pl.ds(slab_base(chain, STATE_TILES) + t, STATE_TILES, stride=SLAB_PITCH)
            nxr = lam_r * xr - lam_i * xi + slab_ref[re_rows, :]
            nxi = lam_r * xi + lam_i * xr + slab_ref[im_rows, :]
            slab_ref[re_rows, :] = nxr
            slab_ref[im_rows, :] = nxi
            new_state += [nxr, nxi]
        return tuple(new_state)

    init = tuple(carry_ref[s] for s in range(2 * len(chains)))
    final = lax.fori_loop(0, ts, step, init, unroll=8)
    for s in range(2 * len(chains)):
        carry_ref[s] = final[s]

    for chain, (ss, b) in enumerate(chains):
        xr = jnp.concatenate(
            [slab_ref[pl.ds(slab_base(chain, tile), ts), :] for tile in range(STATE_TILES)], axis=1)
        xi = jnp.concatenate(
            [slab_ref[pl.ds(slab_base(chain, STATE_TILES + tile), ts), :] for tile in range(STATE_TILES)], axis=1)
        y = (jnp.dot(xr.astype(BF16), cre_ref[ss], preferred_element_type=F32)
             - jnp.dot(xi.astype(BF16), cim_ref[ss], preferred_element_type=F32))
        y = y + d_ref[:, set_cols(ss)] * normed(ss, b)
        o_ref[b, :, set_cols(ss)] = jax.nn.gelu(y).astype(o_ref.dtype)


def _s5_scan(h3, inv3, g, bmat, cre, cim, lam_r, lam_i, d_skip):
    nb, lp, d = h3.shape
    nset = bmat.shape[0]
    per = S5_SETS_PER_STEP
    cols = per * S5_SET_IN
    chains = per * nb
    return pl.pallas_call(
        _s5_scan_body,
        grid=(nset // per, lp // SEQ_BLOCK),
        in_specs=[
            pl.BlockSpec((nb, SEQ_BLOCK, cols), lambda s, c: (0, c, s)),
            pl.BlockSpec((nb, SEQ_BLOCK, LANES), lambda s, c: (0, c, 0)),
            pl.BlockSpec((1, cols), lambda s, c: (0, s)),
            pl.BlockSpec((per, S5_SET_IN, 2 * S5_SET_STATE), lambda s, c: (s, 0, 0)),
            pl.BlockSpec((per, S5_SET_STATE, S5_SET_IN), lambda s, c: (s, 0, 0)),
            pl.BlockSpec((per, S5_SET_STATE, S5_SET_IN), lambda s, c: (s, 0, 0)),
            pl.BlockSpec((per, STATE_TILES, LANES), lambda s, c: (s, 0, 0)),
            pl.BlockSpec((per, STATE_TILES, LANES), lambda s, c: (s, 0, 0)),
            pl.BlockSpec((1, cols), lambda s, c: (0, s)),
        ],
        out_specs=pl.BlockSpec((nb, SEQ_BLOCK, cols), lambda s, c: (0, c, s)),
        out_shape=jax.ShapeDtypeStruct((nb, lp, d), BF16),
        scratch_shapes=[pltpu.VMEM((chains * 2 * STATE_TILES * SLAB_PITCH, LANES), F32),
                        pltpu.VMEM((2 * chains, STATE_TILES, LANES), F32)],
        compiler_params=_params(("parallel", "arbitrary")),
        name="s5_scan",
    )(h3, inv3, g.reshape(1, d), bmat, cre, cim, lam_r, lam_i, d_skip.reshape(1, d))


def _attn_conv_layer(h, g, w_qkv, w_fg, w_gates, w_o, layer, b_f, conv_w, conv_b, batch, lp, pad):
    u = _rmsnorm(h, g[0], BF16)
    qkv = _matmul_slabs(u, w_qkv, layer, 0, w_qkv.shape[2], BF16, 768, "qkv_proj")
    fg = _matmul_slabs(u, w_fg, layer, 0, LANES, F32, LANES, "forget_proj")[0]
    b_pad = jnp.pad(b_f, (0, LANES - ATTN_HEADS)).reshape(1, LANES)
    negc = _neg_cum_logf(fg, b_pad, batch, lp, pad)
    negc = jnp.transpose(negc[:, :ATTN_HEADS].reshape(batch, lp // SEQ_BLOCK, SEQ_BLOCK, HEAD_PAIRS, 2),
                         (0, 3, 1, 4, 2))
    attn = _fox_attn(qkv, negc, batch, lp)
    conv = _gate_conv(u, w_gates, layer, conv_w, conv_b)
    return _out_proj_residual(attn, conv, w_o, layer, h, g[1])


def _s5_layer(h, inv_rms, g, a_re, a_im, log_step, b_re, b_im, c_re, c_im, d_skip, w_glu1, w_glu2, layer, batch, lp):
    m, d = h.shape
    operands = _s5_operands(a_re, a_im, log_step, b_re, b_im, c_re, c_im)
    act = _s5_scan(h.reshape(batch, lp, d), inv_rms.reshape(batch, lp, LANES), g[0], *operands, d_skip)
    return _glu_residual(act.reshape(m, d), w_glu1, w_glu2, layer, h, g[1])


def _projection_weights(w_in):
    qkv_w = 3 * ATTN_WIDTH
    w_all = w_in.astype(BF16)
    w_fg = jnp.pad(w_all[..., qkv_w:qkv_w + ATTN_HEADS], ((0, 0), (0, 0), (0, LANES - ATTN_HEADS)))
    return w_all[..., :qkv_w], w_fg, w_all[..., qkv_w + ATTN_HEADS:]


def kernel(x, meta_tokens, norm_g, ab_w_in, ab_b_f, ab_conv_w, ab_conv_b, ab_w_o, s5_a_re, s5_a_im, s5_log_step, s5_b_re, s5_b_im, s5_c_re, s5_c_im, s5_d, s5_w_glu1, s5_w_glu2, ffn_w_gate, ffn_w_up, ffn_w_down):
    batch, seq, d = x.shape
    depth = norm_g.shape[0]
    tokens = N_META + seq
    lp = pl.cdiv(tokens, SEQ_BLOCK) * SEQ_BLOCK
    assert (batch * lp) % ROW_BLOCK == 0
    pad = lp - tokens
    meta = jnp.broadcast_to(meta_tokens.astype(x.dtype)[None], (batch, N_META, d))
    h = jnp.concatenate([jnp.zeros((batch, pad, d), x.dtype), meta, x], axis=1).reshape(batch * lp, d)
    w_qkv, w_fg, w_gates = _projection_weights(ab_w_in)
    w_o = ab_w_o.astype(BF16)
    w_glu1 = s5_w_glu1.astype(BF16)
    w_glu2 = s5_w_glu2.astype(BF16)
    w_gate = ffn_w_gate.astype(BF16)
    w_up = ffn_w_up.astype(BF16)
    w_down = ffn_w_down
    inv_rms = None
    for i in range(depth):
        g = norm_g[i]
        j = i // 2
        if i % 2 == 0:
            h = _attn_conv_layer(h, g, w_qkv, w_fg, w_gates, w_o, j, ab_b_f[j], ab_conv_w[j], ab_conv_b[j],
                                 batch, lp, pad)
        else:
            h = _s5_layer(h, inv_rms, g, s5_a_re[j], s5_a_im[j], s5_log_step[j], s5_b_re[j], s5_b_im[j],
                          s5_c_re[j], s5_c_im[j], s5_d[j], w_glu1, w_glu2, j, batch, lp)
        next_is_s5 = i + 1 < depth and (i + 1) % 2 == 1
        out = _ffn_residual(h, g[2], g[3], w_gate, w_up, w_down, i, next_is_s5)
        h, inv_rms = out if next_is_s5 else (out, None)
    return h.reshape(batch, lp, d)[:, pad + N_META:]
```

```python
import functools
import math

import jax
import jax.numpy as jnp
from jax import lax
from jax.experimental import pallas as pl
from jax.experimental.pallas import tpu as pltpu

F32 = jnp.float32
BF16 = jnp.bfloat16

N_META = 16
ATTN_HEADS = 16
HEAD_DIM = 64
ATTN_WIDTH = ATTN_HEADS * HEAD_DIM
CONV_K = 3
S5_GROUP = 16
S5_STATE = 64
S5_MIN_DECAY = 1e-4
NORM_EPS = 1e-6

LANES = 128
SUBLANES = 8
ROW_BLOCK = 768
SEQ_BLOCK = 384
HEAD_PAIRS = ATTN_HEADS * HEAD_DIM // LANES
S5_SET_GROUPS = 16
S5_SET_IN = S5_SET_GROUPS * S5_GROUP
S5_SET_STATE = S5_SET_GROUPS * S5_STATE
S5_SETS_PER_STEP = 2
STATE_TILES = S5_SET_STATE // LANES
SLAB_PITCH = SEQ_BLOCK + SUBLANES // 2
KEY_MASK = -1e30
VMEM_LIMIT = 60 * 1024 * 1024


def _params(sem):
    return pltpu.CompilerParams(dimension_semantics=sem, vmem_limit_bytes=VMEM_LIMIT)


def _rmsnorm_body(x_ref, g_ref, o_ref):
    x = x_ref[...]
    ms = jnp.mean(x * x, axis=-1, keepdims=True)
    o_ref[...] = (x * lax.rsqrt(ms + NORM_EPS) * g_ref[...]).astype(o_ref.dtype)


def _rmsnorm(h, g, out_dtype):
    m, d = h.shape
    return pl.pallas_call(
        _rmsnorm_body,
        grid=(m // ROW_BLOCK,),
        in_specs=[pl.BlockSpec((ROW_BLOCK, d), lambda i: (i, 0)),
                  pl.BlockSpec((1, d), lambda i: (0, 0))],
        out_specs=pl.BlockSpec((ROW_BLOCK, d), lambda i: (i, 0)),
        out_shape=jax.ShapeDtypeStruct((m, d), out_dtype),
        compiler_params=_params(("parallel",)),
        name="rmsnorm",
    )(h, g.reshape(1, d))


def _matmul_slabs_body(x_ref, w_ref, o_ref, *, nslab):
    r = jnp.dot(x_ref[...], w_ref[...], preferred_element_type=F32)
    for c in range(nslab):
        o_ref[c] = r[:, c * LANES:(c + 1) * LANES].astype(o_ref.dtype)


def _matmul_slabs(x, w, layer, col0, n, out_dtype, tn, name):
    m, k = x.shape
    nslab = tn // LANES
    assert col0 % tn == 0 and n % tn == 0
    first = col0 // tn
    return pl.pallas_call(
        functools.partial(_matmul_slabs_body, nslab=nslab),
        grid=(m // ROW_BLOCK, n // tn),
        in_specs=[pl.BlockSpec((ROW_BLOCK, k), lambda i, j: (i, 0)),
                  pl.BlockSpec((None, k, tn), lambda i, j: (layer, 0, first + j))],
        out_specs=pl.BlockSpec((nslab, ROW_BLOCK, LANES), lambda i, j: (j, i, 0)),
        out_shape=jax.ShapeDtypeStruct((n // LANES, m, LANES), out_dtype),
        compiler_params=_params(("parallel", "parallel")),
        name=name,
    )(x, w)


def _neg_cum_logf_body(fg_ref, b_ref, o_ref, carry_ref, *, pad):
    c = pl.program_id(1)

    @pl.when(c == 0)
    def _():
        carry_ref[...] = jnp.zeros_like(carry_ref)

    z = fg_ref[...] + b_ref[...]
    logf = jnp.minimum(z, 0.0) - jnp.log1p(jnp.exp(-jnp.abs(z)))
    t = logf.shape[0]
    row = lax.broadcasted_iota(jnp.int32, (t, t), 0)
    col = lax.broadcasted_iota(jnp.int32, (t, t), 1)
    tri = (col <= row).astype(BF16)
    hi = logf.astype(BF16)
    rem = logf - hi.astype(F32)
    mid = rem.astype(BF16)
    lo = (rem - mid.astype(F32)).astype(BF16)
    cs = (jnp.dot(tri, hi, preferred_element_type=F32)
          + jnp.dot(tri, mid, preferred_element_type=F32)
          + jnp.dot(tri, lo, preferred_element_type=F32)) + carry_ref[...]
    carry_ref[...] = cs[t - 1:t, :]
    pos = c * t + lax.broadcasted_iota(jnp.int32, (t, 1), 0)
    o_ref[...] = jnp.where(pos < pad, KEY_MASK, -cs)


def _neg_cum_logf(fg, b_f, batch, lp, pad):
    m = fg.shape[0]
    nblk = lp // SEQ_BLOCK
    return pl.pallas_call(
        functools.partial(_neg_cum_logf_body, pad=pad),
        grid=(batch, nblk),
        in_specs=[pl.BlockSpec((SEQ_BLOCK, LANES), lambda b, c: (b * nblk + c, 0)),
                  pl.BlockSpec((1, LANES), lambda b, c: (0, 0))],
        out_specs=pl.BlockSpec((SEQ_BLOCK, LANES), lambda b, c: (b * nblk + c, 0)),
        out_shape=jax.ShapeDtypeStruct((m, LANES), F32),
        scratch_shapes=[pltpu.VMEM((1, LANES), F32)],
        compiler_params=_params(("parallel", "arbitrary")),
        name="neg_cum_logf",
    )(fg, b_f)


def _fox_attn_body(qtab_ref, ktab_ref, q_ref, k_ref, v_ref, nc_ref, o_ref,
                   qh_ref, vaug_ref, s_ref, p_ref, scale_ref, m_ref, acc_ref, *, nitems):
    nblk, tq = vaug_ref.shape[0], s_ref.shape[2]
    lane = lax.broadcasted_iota(jnp.int32, (1, LANES), 1)
    first_head = lane < HEAD_DIM

    ones_a = jnp.broadcast_to((lane == 0).astype(BF16), (tq, LANES))
    ones_b = jnp.broadcast_to((lane == 1).astype(BF16), (tq, LANES))
    for blk in range(nblk):
        rows = slice(blk * tq, (blk + 1) * tq)
        v = v_ref[rows, :]
        q = q_ref[rows, :]
        zero = jnp.zeros_like(v)
        vaug_ref[blk, 0:tq, 0:LANES] = jnp.where(first_head, v, zero)
        vaug_ref[blk, tq:2 * tq, 0:LANES] = jnp.where(first_head, zero, v)
        vaug_ref[blk, 0:tq, LANES:2 * LANES] = ones_a
        vaug_ref[blk, tq:2 * tq, LANES:2 * LANES] = ones_b
        qh_ref[blk, 0:tq, :] = jnp.where(first_head, q, zero) * (HEAD_DIM ** -0.5)
        qh_ref[blk, tq:2 * tq, :] = jnp.where(first_head, zero, q) * (HEAD_DIM ** -0.5)
    m_ref[...] = jnp.full_like(m_ref, -jnp.inf)
    acc_ref[...] = jnp.zeros_like(acc_ref)
    p_ref[1] = jnp.zeros(p_ref.shape[1:], p_ref.dtype)
    scale_ref[1] = jnp.ones(scale_ref.shape[1:], scale_ref.dtype)

    def scores(n, slot):
        qi = qtab_ref[n]
        kb = ktab_ref[n]
        rows = pl.ds(pl.multiple_of(kb * tq, tq), tq)
        s = lax.dot_general(qh_ref[qi], k_ref[rows, :], (((1,), (1,)), ((), ())),
                            preferred_element_type=F32)
        nc = nc_ref[kb]
        s_ref[slot, 0:tq, :] = s[0:tq] + nc[0:1, :]
        s_ref[slot, tq:2 * tq, :] = s[tq:2 * tq] + nc[1:2, :]

    def softmax(n, slot, diagonal):
        qi = qtab_ref[n]
        s = s_ref[slot]
        if diagonal:
            row = lax.broadcasted_iota(jnp.int32, (2 * tq, tq), 0)
            row = jnp.where(row >= tq, row - tq, row)
            col = lax.broadcasted_iota(jnp.int32, (2 * tq, tq), 1)
            s = jnp.where(col <= row, s, -jnp.inf)
        m_prev = m_ref[qi]
        m_next = jnp.maximum(m_prev, jnp.max(s, axis=-1, keepdims=True))
        m_ref[qi] = m_next
        alpha = jnp.exp(m_prev - m_next)
        p = jnp.exp(s - jnp.concatenate([m_next] * (tq // LANES), axis=1)).astype(BF16)
        p_ref[slot, :, 0:tq] = p[0:tq]
        p_ref[slot, :, tq:2 * tq] = p[tq:2 * tq]
        alpha_a = alpha[0:tq]
        alpha_b = alpha[tq:2 * tq]
        scale_ref[slot, :, 0:LANES] = jnp.where(first_head, alpha_a, alpha_b)
        scale_ref[slot, :, LANES:2 * LANES] = jnp.where(lane == 0, alpha_a, alpha_b)

    def accumulate(n, slot):
        qi = qtab_ref[n]
        pv = jnp.dot(p_ref[slot], vaug_ref[ktab_ref[n]], preferred_element_type=F32)
        acc_ref[qi] = scale_ref[slot] * acc_ref[qi] + pv

    def pair(j, diagonal0, diagonal1):
        n0 = 2 * j
        accumulate(jnp.maximum(n0 - 1, 0), 1)
        softmax(n0, 0, diagonal0)
        scores(n0 + 1, 1)
        accumulate(n0, 0)
        softmax(n0 + 1, 1, diagonal1)
        scores(n0 + 2, 0)

    def pairs(lo, hi, diagonal):
        def body(j, carry):
            pair(j, diagonal, diagonal)
            return carry
        lax.fori_loop(lo, hi, body, 0)

    scores(0, 0)
    pairs(0, nblk // 2, True)
    if nblk % 2:
        pair(nblk // 2, True, False)
    pairs((nblk + 1) // 2, nitems // 2, False)
    accumulate(nitems - 1, 1)

    for blk in range(nblk):
        acc = acc_ref[blk]
        sum_a = jnp.broadcast_to(acc[:, LANES:LANES + 1], (tq, LANES))
        sum_b = jnp.broadcast_to(acc[:, LANES + 1:LANES + 2], (tq, LANES))
        o_ref[blk * tq:(blk + 1) * tq, :] = (
            acc[:, 0:LANES] / jnp.where(first_head, sum_a, sum_b)).astype(o_ref.dtype)


def _fox_attn(qkv, negc, batch, lp):
    nslab, m, _ = qkv.shape
    nq = lp // SEQ_BLOCK
    items = [(qi, qi - d) for d in range(nq) for qi in range(d, nq)]
    nitems = len(items)
    assert nitems % 2 == 0
    items.append(items[-1])
    qtab = jnp.asarray([it[0] for it in items], jnp.int32)
    ktab = jnp.asarray([it[1] for it in items], jnp.int32)
    qkv_seq = qkv.reshape(nslab, batch, lp, LANES)
    seq_blk = (None, None, lp, LANES)
    grid_spec = pltpu.PrefetchScalarGridSpec(
        num_scalar_prefetch=2,
        grid=(batch, HEAD_PAIRS),
        in_specs=[
            pl.BlockSpec(seq_blk, lambda b, p, qt, kt: (p, b, 0, 0)),
            pl.BlockSpec(seq_blk, lambda b, p, qt, kt: (HEAD_PAIRS + p, b, 0, 0)),
            pl.BlockSpec(seq_blk, lambda b, p, qt, kt: (2 * HEAD_PAIRS + p, b, 0, 0)),
            pl.BlockSpec((None, None, nq, 2, SEQ_BLOCK), lambda b, p, qt, kt: (b, p, 0, 0, 0)),
        ],
        out_specs=pl.BlockSpec((lp, LANES), lambda b, p, qt, kt: (b, p)),
        scratch_shapes=[pltpu.VMEM((nq, 2 * SEQ_BLOCK, LANES), BF16),
                        pltpu.VMEM((nq, 2 * SEQ_BLOCK, 2 * LANES), BF16),
                        pltpu.VMEM((2, 2 * SEQ_BLOCK, SEQ_BLOCK), F32),
                        pltpu.VMEM((2, SEQ_BLOCK, 2 * SEQ_BLOCK), BF16),
                        pltpu.VMEM((2, SEQ_BLOCK, 2 * LANES), F32),
                        pltpu.VMEM((nq, 2 * SEQ_BLOCK, LANES), F32),
                        pltpu.VMEM((nq, SEQ_BLOCK, 2 * LANES), F32)],
    )
    return pl.pallas_call(
        functools.partial(_fox_attn_body, nitems=nitems),
        grid_spec=grid_spec,
        out_shape=jax.ShapeDtypeStruct((m, ATTN_WIDTH), BF16),
        compiler_params=_params(("parallel", "parallel")),
        name="fox_attn",
    )(qtab, ktab, qkv_seq, qkv_seq, qkv_seq, negc)


CONV_HALO = 16


def _gate_conv_body(x_ref, xh_ref, wb_ref, wc_ref, wx_ref, cw_ref, cb_ref, o_ref, xcat_ref, z_ref):
    i = pl.program_id(0)
    j = pl.program_id(1)
    tm = x_ref.shape[0]

    @pl.when(j == 0)
    def _():
        xcat_ref[0:CONV_HALO, :] = xh_ref[...]
        xcat_ref[CONV_HALO:CONV_HALO + tm, :] = x_ref[...]

    xcat = xcat_ref[...]
    gate_b = jnp.dot(xcat, wb_ref[...], preferred_element_type=F32)
    gate_c = jnp.dot(xcat, wc_ref[...], preferred_element_type=F32)
    conv_in = jnp.dot(xcat, wx_ref[...], preferred_element_type=F32)
    z_all = gate_c * conv_in
    z_ref[...] = z_all
    head = z_all[0:CONV_HALO]
    z_ref[0:CONV_HALO, :] = jnp.where(i > 0, head, jnp.zeros_like(head))
    z = z_all[CONV_HALO:CONV_HALO + tm]
    z1 = z_ref[CONV_HALO - 1:CONV_HALO - 1 + tm, :]
    z2 = z_ref[CONV_HALO - 2:CONV_HALO - 2 + tm, :]
    conv = cw_ref[0:1, :] * z2 + cw_ref[1:2, :] * z1 + cw_ref[2:3, :] * z + cb_ref[...]
    o_ref[...] = (gate_b[CONV_HALO:CONV_HALO + tm] * conv).astype(o_ref.dtype)


def _gate_conv(x, w, layer, conv_w, conv_b, nch=256):
    m, k = x.shape
    channels = w.shape[2] // 3
    nblk = channels // nch
    halo_per_block = ROW_BLOCK // CONV_HALO

    def w_spec(kind):
        return pl.BlockSpec((None, k, nch), lambda i, j: (layer, 0, kind * nblk + j))

    return pl.pallas_call(
        _gate_conv_body,
        grid=(m // ROW_BLOCK, nblk),
        in_specs=[
            pl.BlockSpec((ROW_BLOCK, k), lambda i, j: (i, 0)),
            pl.BlockSpec((CONV_HALO, k), lambda i, j: (jnp.maximum(i * halo_per_block - 1, 0), 0)),
            w_spec(0), w_spec(1), w_spec(2),
            pl.BlockSpec((CONV_K, nch), lambda i, j: (0, j)),
            pl.BlockSpec((1, nch), lambda i, j: (0, j)),
        ],
        out_specs=pl.BlockSpec((ROW_BLOCK, nch), lambda i, j: (i, j)),
        out_shape=jax.ShapeDtypeStruct((m, channels), BF16),
        scratch_shapes=[pltpu.VMEM((ROW_BLOCK + CONV_HALO, k), BF16),
                        pltpu.VMEM((ROW_BLOCK + CONV_HALO, nch), F32)],
        compiler_params=_params(("parallel", "arbitrary")),
        name="gate_conv",
    )(x, x, w, w, w, conv_w, conv_b.reshape(1, -1))


def _residual_norm_store(m_ref, h_ref, g_ref, o_ref):
    nchunk, _, tn = m_ref.shape
    ss = None
    for c in range(nchunk):
        mc = m_ref[c]
        part = jnp.sum(mc * mc, axis=-1, keepdims=True)
        ss = part if ss is None else ss + part
    scale = lax.rsqrt(ss / (nchunk * tn) + NORM_EPS)
    for c in range(nchunk):
        cols = slice(c * tn, (c + 1) * tn)
        o_ref[:, cols] = h_ref[:, cols] + m_ref[c] * scale * g_ref[:, cols]


def _out_proj_body(xa_ref, xb_ref, w_ref, h_ref, g_ref, o_ref, m_ref):
    nchunk, _, tn = m_ref.shape
    ka = xa_ref.shape[1]
    xa = xa_ref[...]
    xb = xb_ref[...]
    for c in range(nchunk):
        cols = slice(c * tn, (c + 1) * tn)
        m_ref[c] = (jnp.dot(xa, w_ref[0:ka, cols], preferred_element_type=F32)
                    + jnp.dot(xb, w_ref[ka:, cols], preferred_element_type=F32))
    _residual_norm_store(m_ref, h_ref, g_ref, o_ref)


def _out_proj_residual(xa, xb, w, layer, h, g, tn=512):
    m, ka = xa.shape
    kb = xb.shape[1]
    n = w.shape[2]
    return pl.pallas_call(
        _out_proj_body,
        grid=(m // ROW_BLOCK,),
        in_specs=[
            pl.BlockSpec((ROW_BLOCK, ka), lambda i: (i, 0)),
            pl.BlockSpec((ROW_BLOCK, kb), lambda i: (i, 0)),
            pl.BlockSpec((None, ka + kb, n), lambda i: (layer, 0, 0)),
            pl.BlockSpec((ROW_BLOCK, n), lambda i: (i, 0)),
            pl.BlockSpec((1, n), lambda i: (0, 0)),
        ],
        out_specs=pl.BlockSpec((ROW_BLOCK, n), lambda i: (i, 0)),
        out_shape=jax.ShapeDtypeStruct((m, n), F32),
        scratch_shapes=[pltpu.VMEM((n // tn, ROW_BLOCK, tn), F32)],
        compiler_params=_params(("parallel",)),
        name="out_proj_residual",
    )(xa, xb, w, h, g.reshape(1, n))


def _glu_body(x_ref, w1_ref, w2_ref, h_ref, g_ref, o_ref, m_ref):
    nchunk, _, tn = m_ref.shape
    x = x_ref[...]
    for c in range(nchunk):
        cols = slice(c * tn, (c + 1) * tn)
        a = jnp.dot(x, w1_ref[:, cols], preferred_element_type=F32)
        b = jnp.dot(x, w2_ref[:, cols], preferred_element_type=F32)
        m_ref[c] = a * jax.nn.sigmoid(b)
    _residual_norm_store(m_ref, h_ref, g_ref, o_ref)


def _glu_residual(x, w1, w2, layer, h, g, tn=512):
    m, k = x.shape
    n = w1.shape[2]
    w_spec = pl.BlockSpec((None, k, n), lambda i: (layer, 0, 0), pipeline_mode=pl.Buffered(1))
    return pl.pallas_call(
        _glu_body,
        grid=(m // ROW_BLOCK,),
        in_specs=[
            pl.BlockSpec((ROW_BLOCK, k), lambda i: (i, 0)),
            w_spec, w_spec,
            pl.BlockSpec((ROW_BLOCK, n), lambda i: (i, 0)),
            pl.BlockSpec((1, n), lambda i: (0, 0)),
        ],
        out_specs=pl.BlockSpec((ROW_BLOCK, n), lambda i: (i, 0)),
        out_shape=jax.ShapeDtypeStruct((m, n), F32),
        scratch_shapes=[pltpu.VMEM((n // tn, ROW_BLOCK, tn), F32)],
        compiler_params=_params(("parallel",)),
        name="glu_residual",
    )(x, w1, w2, h, g.reshape(1, n))


def _ffn_body(h_ref, gin_ref, gout_ref, wg_ref, wu_ref, wd_ref, o_ref, *rest, emit_inv_rms):
    inv_ref = rest[0] if emit_inv_rms else None
    hn_ref, acc_ref = rest[-2:]
    j = pl.program_id(1)

    @pl.when(j == 0)
    def _():
        x = h_ref[...]
        ms = jnp.mean(x * x, axis=-1, keepdims=True)
        hn_ref[...] = (x * lax.rsqrt(ms + NORM_EPS) * gin_ref[...]).astype(hn_ref.dtype)
        acc_ref[...] = jnp.zeros_like(acc_ref)

    hn = hn_ref[...]
    gate = jnp.dot(hn, wg_ref[...], preferred_element_type=F32)
    up = jnp.dot(hn, wu_ref[...], preferred_element_type=F32)
    act = (gate * jax.nn.sigmoid(gate) * up).astype(BF16)
    acc_ref[...] += jnp.dot(act, wd_ref[...].astype(BF16), preferred_element_type=F32)

    @pl.when(j == pl.num_programs(1) - 1)
    def _():
        f = acc_ref[...]
        ms = jnp.mean(f * f, axis=-1, keepdims=True)
        out = h_ref[...] + f * lax.rsqrt(ms + NORM_EPS) * gout_ref[...]
        o_ref[...] = out
        if emit_inv_rms:
            ms_out = jnp.mean(out * out, axis=-1, keepdims=True)
            inv_ref[...] = jnp.broadcast_to(lax.rsqrt(ms_out + NORM_EPS), inv_ref.shape)


def _ffn_residual(h, g_in, g_out, w_gate, w_up, w_down, layer, emit_inv_rms, th=512):
    m, d = h.shape
    hidden = w_gate.shape[2]
    row_spec = pl.BlockSpec((ROW_BLOCK, d), lambda i, j: (i, 0))
    out_specs = [row_spec]
    out_shape = [jax.ShapeDtypeStruct((m, d), F32)]
    if emit_inv_rms:
        out_specs.append(pl.BlockSpec((ROW_BLOCK, LANES), lambda i, j: (i, 0)))
        out_shape.append(jax.ShapeDtypeStruct((m, LANES), F32))
    outs = pl.pallas_call(
        functools.partial(_ffn_body, emit_inv_rms=emit_inv_rms),
        grid=(m // ROW_BLOCK, hidden // th),
        in_specs=[
            row_spec,
            pl.BlockSpec((1, d), lambda i, j: (0, 0)),
            pl.BlockSpec((1, d), lambda i, j: (0, 0)),
            pl.BlockSpec((None, d, th), lambda i, j: (layer, 0, j)),
            pl.BlockSpec((None, d, th), lambda i, j: (layer, 0, j)),
            pl.BlockSpec((None, th, d), lambda i, j: (layer, j, 0)),
        ],
        out_specs=out_specs,
        out_shape=out_shape,
        scratch_shapes=[pltpu.VMEM((ROW_BLOCK, d), BF16),
                        pltpu.VMEM((ROW_BLOCK, d), F32)],
        compiler_params=_params(("parallel", "arbitrary")),
        name="ffn_residual",
    )(h, g_in.reshape(1, d), g_out.reshape(1, d), w_gate, w_up, w_down)
    return tuple(outs) if emit_inv_rms else outs[0]


def _lane_repeat(x, reps, exact_f32):
    n = x.shape[1]
    row = lax.broadcasted_iota(jnp.int32, (n, n * reps), 0)
    col = lax.broadcasted_iota(jnp.int32, (n, n * reps), 1)
    sel = (jnp.bitwise_and(col, n - 1) == row).astype(BF16)
    hi = x.astype(BF16)
    out = jnp.dot(hi, sel, preferred_element_type=F32)
    if exact_f32:
        rem = x - hi.astype(F32)
        mid = rem.astype(BF16)
        lo = (rem - mid.astype(F32)).astype(BF16)
        out = out + jnp.dot(mid, sel, preferred_element_type=F32) + jnp.dot(lo, sel, preferred_element_type=F32)
    return out


def _diag_block_mask(shape, row_shift, col_shift):
    row = lax.broadcasted_iota(jnp.int32, shape, 0)
    col = lax.broadcasted_iota(jnp.int32, shape, 1)
    return jnp.right_shift(row, row_shift) == jnp.right_shift(col, col_shift)


def _s5_operands_body(are_ref, aim_ref, ls_ref, bre_ref, bim_ref, cre_ref, cim_ref,
                      lbr_ref, lbi_ref, bmat_ref, cmr_ref, cmi_ref):
    lam_re = jnp.minimum(are_ref[...], -S5_MIN_DECAY)
    lam_im = aim_ref[...]
    delta = jnp.exp(ls_ref[...])
    mag = jnp.exp(lam_re * delta)
    ang = lam_im * delta
    lb_re = mag * jnp.cos(ang)
    lb_im = mag * jnp.sin(ang)
    den = lam_re * lam_re + lam_im * lam_im
    nr = lb_re - 1.0
    ni = lb_im
    coef_re = (nr * lam_re + ni * lam_im) / den
    coef_im = (ni * lam_re - nr * lam_im) / den
    lbr_ref[...] = lb_re
    lbi_ref[...] = lb_im

    br = _lane_repeat(bre_ref[...], S5_SET_GROUPS, True)
    bi = _lane_repeat(bim_ref[...], S5_SET_GROUPS, True)
    in_mask = _diag_block_mask(br.shape, S5_GROUP.bit_length() - 1, S5_STATE.bit_length() - 1)
    zero = jnp.zeros_like(br)
    bmat_ref[:, 0:S5_SET_STATE] = jnp.where(in_mask, coef_re * br - coef_im * bi, zero).astype(bmat_ref.dtype)
    bmat_ref[:, S5_SET_STATE:2 * S5_SET_STATE] = jnp.where(
        in_mask, coef_re * bi + coef_im * br, zero).astype(bmat_ref.dtype)

    cr = _lane_repeat(cre_ref[...], S5_SET_GROUPS, False)
    ci = _lane_repeat(cim_ref[...], S5_SET_GROUPS, False)
    out_mask = _diag_block_mask(cr.shape, S5_STATE.bit_length() - 1, S5_GROUP.bit_length() - 1)
    cmr_ref[...] = jnp.where(out_mask, cr, jnp.zeros_like(cr)).astype(cmr_ref.dtype)
    cmi_ref[...] = jnp.where(out_mask, ci, jnp.zeros_like(ci)).astype(cmi_ref.dtype)


def _s5_operands(a_re, a_im, log_step, b_re, b_im, c_re, c_im):
    groups, state = a_re.shape
    nset = groups // S5_SET_GROUPS
    lane_row = lambda a: a.reshape(nset, 1, S5_SET_STATE)
    step_row = lane_row(jnp.broadcast_to(log_step[:, None], (groups, state)))
    b_rows = lambda b: jnp.transpose(b, (0, 2, 1)).reshape(nset, S5_SET_IN, state)
    c_rows = lambda c: jnp.transpose(c, (0, 2, 1)).reshape(nset, S5_SET_STATE, S5_GROUP)
    row_spec = pl.BlockSpec((None, 1, S5_SET_STATE), lambda s: (s, 0, 0))
    b_spec = pl.BlockSpec((None, S5_SET_IN, state), lambda s: (s, 0, 0))
    c_spec = pl.BlockSpec((None, S5_SET_STATE, S5_GROUP), lambda s: (s, 0, 0))
    cm_spec = pl.BlockSpec((None, S5_SET_STATE, S5_SET_IN), lambda s: (s, 0, 0))
    lb_re, lb_im, bmat, cmr, cmi = pl.pallas_call(
        _s5_operands_body,
        grid=(nset,),
        in_specs=[row_spec, row_spec, row_spec, b_spec, b_spec, c_spec, c_spec],
        out_specs=[row_spec, row_spec,
                   pl.BlockSpec((None, S5_SET_IN, 2 * S5_SET_STATE), lambda s: (s, 0, 0)), cm_spec, cm_spec],
        out_shape=[jax.ShapeDtypeStruct((nset, 1, S5_SET_STATE), F32)] * 2
        + [jax.ShapeDtypeStruct((nset, S5_SET_IN, 2 * S5_SET_STATE), BF16)]
        + [jax.ShapeDtypeStruct((nset, S5_SET_STATE, S5_SET_IN), BF16)] * 2,
        compiler_params=_params(("parallel",)),
        name="s5_operands",
    )(lane_row(a_re), lane_row(a_im), step_row, b_rows(b_re), b_rows(b_im), c_rows(c_re), c_rows(c_im))
    lam_r = lb_re.reshape(nset, STATE_TILES, LANES)
    lam_i = lb_im.reshape(nset, STATE_TILES, LANES)
    return bmat, cmr, cmi, lam_r, lam_i


def _s5_scan_body(h_ref, inv_ref, g_ref, bmat_ref, cre_ref, cim_ref, lr_ref, li_ref, d_ref, o_ref, slab_ref,
                  carry_ref):
    c = pl.program_id(1)
    nb, ts, _ = h_ref.shape
    nsets = bmat_ref.shape[0]
    chains = [(ss, b) for ss in range(nsets) for b in range(nb)]

    @pl.when(c == 0)
    def _():
        carry_ref[...] = jnp.zeros_like(carry_ref)

    def slab_base(chain, tile):
        return (chain * 2 * STATE_TILES + tile) * SLAB_PITCH

    def set_cols(ss):
        return slice(ss * S5_SET_IN, (ss + 1) * S5_SET_IN)

    def normed(ss, b):
        inv = jnp.concatenate([inv_ref[b]] * (S5_SET_IN // LANES), axis=1)
        return h_ref[b, :, set_cols(ss)] * inv * g_ref[:, set_cols(ss)]

    for chain, (ss, b) in enumerate(chains):
        bu = jnp.dot(normed(ss, b).astype(BF16), bmat_ref[ss], preferred_element_type=F32)
        for tile in range(2 * STATE_TILES):
            slab_ref[pl.ds(slab_base(chain, tile), ts), :] = bu[:, tile * LANES:(tile + 1) * LANES]

    lam = [(lr_ref[ss], li_ref[ss]) for ss in range(nsets)]

    def step(t, state):
        new_state = []
        for chain, (ss, b) in enumerate(chains):
            lam_r, lam_i = lam[ss]
            xr, xi = state[2 * chain], state[2 * chain + 1]
            re_rows = pl.ds(slab_base(chain, 0) + t, STATE_TILES, stride=SLAB_PITCH)
            im_rows = pl.ds(slab_base(chain, STATE_TILES) + t, STATE_TILES, stride=SLAB_PITCH)
            nxr = lam_r * xr - lam_i * xi + slab_ref[re_rows, :]
            nxi = lam_r * xi + lam_i * xr + slab_ref[im_rows, :]
            slab_ref[re_rows, :] = nxr
            slab_ref[im_rows, :] = nxi
            new_state += [nxr, nxi]
        return tuple(new_state)

    init = tuple(carry_ref[s] for s in range(2 * len(chains)))
    final = lax.fori_loop(0, ts, step, init, unroll=8)
    for s in range(2 * len(chains)):
        carry_ref[s] = final[s]

    for chain, (ss, b) in enumerate(chains):
        xr = jnp.concatenate(
            [slab_ref[pl.ds(slab_base(chain, tile), ts), :] for tile in range(STATE_TILES)], axis=1)
        xi = jnp.concatenate(
            [slab_ref[pl.ds(slab_base(chain, STATE_TILES + tile), ts), :] for tile in range(STATE_TILES)], axis=1)
        y = (jnp.dot(xr.astype(BF16), cre_ref[ss], preferred_element_type=F32)
             - jnp.dot(xi.astype(BF16), cim_ref[ss], preferred_element_type=F32))
        y = y + d_ref[:, set_cols(ss)] * normed(ss, b)
        o_ref[b, :, set_cols(ss)] = jax.nn.gelu(y).astype(o_ref.dtype)


def _s5_scan(h3, inv3, g, bmat, cre, cim, lam_r, lam_i, d_skip):
    nb, lp, d = h3.shape
    nset = bmat.shape[0]
    per = S5_SETS_PER_STEP
    cols = per * S5_SET_IN
    chains = per * nb
    return pl.pallas_call(
        _s5_scan_body,
        grid=(nset // per, lp // SEQ_BLOCK),
        in_specs=[
            pl.BlockSpec((nb, SEQ_BLOCK, cols), lambda s, c: (0, c, s)),
            pl.BlockSpec((nb, SEQ_BLOCK, LANES), lambda s, c: (0, c, 0)),
            pl.BlockSpec((1, cols), lambda s, c: (0, s)),
            pl.BlockSpec((per, S5_SET_IN, 2 * S5_SET_STATE), lambda s, c: (s, 0, 0)),
            pl.BlockSpec((per, S5_SET_STATE, S5_SET_IN), lambda s, c: (s, 0, 0)),
            pl.BlockSpec((per, S5_SET_STATE, S5_SET_IN), lambda s, c: (s, 0, 0)),
            pl.BlockSpec((per, STATE_TILES, LANES), lambda s, c: (s, 0, 0)),
            pl.BlockSpec((per, STATE_TILES, LANES), lambda s, c: (s, 0, 0)),
            pl.BlockSpec((1, cols), lambda s, c: (0, s)),
        ],
        out_specs=pl.BlockSpec((nb, SEQ_BLOCK, cols), lambda s, c: (0, c, s)),
        out_shape=jax.ShapeDtypeStruct((nb, lp, d), BF16),
        scratch_shapes=[pltpu.VMEM((chains * 2 * STATE_TILES * SLAB_PITCH, LANES), F32),
                        pltpu.VMEM((2 * chains, STATE_TILES, LANES), F32)],
        compiler_params=_params(("parallel", "arbitrary")),
        name="s5_scan",
    )(h3, inv3, g.reshape(1, d), bmat, cre, cim, lam_r, lam_i, d_skip.reshape(1, d))


def _attn_conv_layer(h, g, w_qkv, w_fg, w_gates, w_o, layer, b_f, conv_w, conv_b, batch, lp, pad):
    u = _rmsnorm(h, g[0], BF16)
    qkv = _matmul_slabs(u, w_qkv, layer, 0, w_qkv.shape[2], BF16, 768, "qkv_proj")
    fg = _matmul_slabs(u, w_fg, layer, 0, LANES, F32, LANES, "forget_proj")[0]
    b_pad = jnp.pad(b_f, (0, LANES - ATTN_HEADS)).reshape(1, LANES)
    negc = _neg_cum_logf(fg, b_pad, batch, lp, pad)
    negc = jnp.transpose(negc[:, :ATTN_HEADS].reshape(batch, lp // SEQ_BLOCK, SEQ_BLOCK, HEAD_PAIRS, 2),
                         (0, 3, 1, 4, 2))
    attn = _fox_attn(qkv, negc, batch, lp)
    conv = _gate_conv(u, w_gates, layer, conv_w, conv_b)
    return _out_proj_residual(attn, conv, w_o, layer, h, g[1])


def _s5_layer(h, inv_rms, g, a_re, a_im, log_step, b_re, b_im, c_re, c_im, d_skip, w_glu1, w_glu2, layer, batch, lp):
    m, d = h.shape
    operands = _s5_operands(a_re, a_im, log_step, b_re, b_im, c_re, c_im)
    act = _s5_scan(h.reshape(batch, lp, d), inv_rms.reshape(batch, lp, LANES), g[0], *operands, d_skip)
    return _glu_residual(act.reshape(m, d), w_glu1, w_glu2, layer, h, g[1])


def _projection_weights(w_in):
    qkv_w = 3 * ATTN_WIDTH
    w_all = w_in.astype(BF16)
    w_fg = jnp.pad(w_all[..., qkv_w:qkv_w + ATTN_HEADS], ((0, 0), (0, 0), (0, LANES - ATTN_HEADS)))
    return w_all[..., :qkv_w], w_fg, w_all[..., qkv_w + ATTN_HEADS:]


def kernel(x, meta_tokens, norm_g, ab_w_in, ab_b_f, ab_conv_w, ab_conv_b, ab_w_o, s5_a_re, s5_a_im, s5_log_step, s5_b_re, s5_b_im, s5_c_re, s5_c_im, s5_d, s5_w_glu1, s5_w_glu2, ffn_w_gate, ffn_w_up, ffn_w_down):
    batch, seq, d = x.shape
    depth = norm_g.shape[0]
    tokens = N_META + seq
    lp = pl.cdiv(tokens, SEQ_BLOCK) * SEQ_BLOCK
    assert (batch * lp) % ROW_BLOCK == 0
    pad = lp - tokens
    meta = jnp.broadcast_to(meta_tokens.astype(x.dtype)[None], (batch, N_META, d))
    h = jnp.concatenate([jnp.zeros((batch, pad, d), x.dtype), meta, x], axis=1).reshape(batch * lp, d)
    w_qkv, w_fg, w_gates = _projection_weights(ab_w_in)
    w_o = ab_w_o.astype(BF16)
    w_glu1 = s5_w_glu1.astype(BF16)
    w_glu2 = s5_w_glu2.astype(BF16)
    w_gate = ffn_w_gate.astype(BF16)
    w_up = ffn_w_up.astype(BF16)
    w_down = ffn_w_down
    inv_rms = None
    for i in range(depth):
        g = norm_g[i]
        j = i // 2
        if i % 2 == 0:
            h = _attn_conv_layer(h, g, w_qkv, w_fg, w_gates, w_o, j, ab_b_f[j], ab_conv_w[j], ab_conv_b[j],
                                 batch, lp, pad)
        else:
            h = _s5_layer(h, inv_rms, g, s5_a_re[j], s5_a_im[j], s5_log_step[j], s5_b_re[j], s5_b_im[j],
                          s5_c_re[j], s5_c_im[j], s5_d[j], w_glu1, w_glu2, j, batch, lp)
        next_is_s5 = i + 1 < depth and (i + 1) % 2 == 1
        out = _ffn_residual(h, g[2], g[3], w_gate, w_up, w_down, i, next_is_s5)
        h, inv_rms = out if next_is_s5 else (out, None)
    return h.reshape(batch, lp, d)[:, pad + N_META:]
```

```python
import functools
import math

import jax
import jax.numpy as jnp
from jax import lax
from jax.experimental import pallas as pl
from jax.experimental.pallas import tpu as pltpu

F32 = jnp.float32
BF16 = jnp.bfloat16

N_META = 16
ATTN_HEADS = 16
HEAD_DIM = 64
ATTN_WIDTH = ATTN_HEADS * HEAD_DIM
CONV_K = 3
S5_GROUP = 16
S5_STATE = 64
S5_MIN_DECAY = 1e-4
NORM_EPS = 1e-6

LANES = 128
SUBLANES = 8
ROW_BLOCK = 768
SEQ_BLOCK = 384
HEAD_PAIRS = ATTN_HEADS * HEAD_DIM // LANES
S5_SET_GROUPS = 16
S5_SET_IN = S5_SET_GROUPS * S5_GROUP
S5_SET_STATE = S5_SET_GROUPS * S5_STATE
S5_SETS_PER_STEP = 2
STATE_TILES = S5_SET_STATE // LANES
SLAB_PITCH = SEQ_BLOCK + SUBLANES // 2
KEY_MASK = -1e30
VMEM_LIMIT = 60 * 1024 * 1024


def _params(sem):
    return pltpu.CompilerParams(dimension_semantics=sem, vmem_limit_bytes=VMEM_LIMIT)


def _rmsnorm_body(x_ref, g_ref, o_ref):
    x = x_ref[...]
    ms = jnp.mean(x * x, axis=-1, keepdims=True)
    o_ref[...] = (x * lax.rsqrt(ms + NORM_EPS) * g_ref[...]).astype(o_ref.dtype)


def _rmsnorm(h, g, out_dtype):
    m, d = h.shape
    return pl.pallas_call(
        _rmsnorm_body,
        grid=(m // ROW_BLOCK,),
        in_specs=[pl.BlockSpec((ROW_BLOCK, d), lambda i: (i, 0)),
                  pl.BlockSpec((1, d), lambda i: (0, 0))],
        out_specs=pl.BlockSpec((ROW_BLOCK, d), lambda i: (i, 0)),
        out_shape=jax.ShapeDtypeStruct((m, d), out_dtype),
        compiler_params=_params(("parallel",)),
        name="rmsnorm",
    )(h, g.reshape(1, d))


def _matmul_slabs_body(x_ref, w_ref, o_ref, *, nslab):
    r = jnp.dot(x_ref[...], w_ref[...], preferred_element_type=F32)
    for c in range(nslab):
        o_ref[c] = r[:, c * LANES:(c + 1) * LANES].astype(o_ref.dtype)


def _matmul_slabs(x, w, layer, col0, n, out_dtype, tn, name):
    m, k = x.shape
    nslab = tn // LANES
    assert col0 % tn == 0 and n % tn == 0
    first = col0 // tn
    return pl.pallas_call(
        functools.partial(_matmul_slabs_body, nslab=nslab),
        grid=(m // ROW_BLOCK, n // tn),
        in_specs=[pl.BlockSpec((ROW_BLOCK, k), lambda i, j: (i, 0)),
                  pl.BlockSpec((None, k, tn), lambda i, j: (layer, 0, first + j))],
        out_specs=pl.BlockSpec((nslab, ROW_BLOCK, LANES), lambda i, j: (j, i, 0)),
        out_shape=jax.ShapeDtypeStruct((n // LANES, m, LANES), out_dtype),
        compiler_params=_params(("parallel", "parallel")),
        name=name,
    )(x, w)


def _neg_cum_logf_body(fg_ref, b_ref, o_ref, carry_ref, *, pad):
    c = pl.program_id(1)

    @pl.when(c == 0)
    def _():
        carry_ref[...] = jnp.zeros_like(carry_ref)

    z = fg_ref[...] + b_ref[...]
    logf = jnp.minimum(z, 0.0) - jnp.log1p(jnp.exp(-jnp.abs(z)))
    t = logf.shape[0]
    row = lax.broadcasted_iota(jnp.int32, (t, t), 0)
    col = lax.broadcasted_iota(jnp.int32, (t, t), 1)
    tri = (col <= row).astype(BF16)
    hi = logf.astype(BF16)
    rem = logf - hi.astype(F32)
    mid = rem.astype(BF16)
    lo = (rem - mid.astype(F32)).astype(BF16)
    cs = (jnp.dot(tri, hi, preferred_element_type=F32)
          + jnp.dot(tri, mid, preferred_element_type=F32)
          + jnp.dot(tri, lo, preferred_element_type=F32)) + carry_ref[...]
    carry_ref[...] = cs[t - 1:t, :]
    pos = c * t + lax.broadcasted_iota(jnp.int32, (t, 1), 0)
    o_ref[...] = jnp.where(pos < pad, KEY_MASK, -cs)


def _neg_cum_logf(fg, b_f, batch, lp, pad):
    m = fg.shape[0]
    nblk = lp // SEQ_BLOCK
    return pl.pallas_call(
        functools.partial(_neg_cum_logf_body, pad=pad),
        grid=(batch, nblk),
        in_specs=[pl.BlockSpec((SEQ_BLOCK, LANES), lambda b, c: (b * nblk + c, 0)),
                  pl.BlockSpec((1, LANES), lambda b, c: (0, 0))],
        out_specs=pl.BlockSpec((SEQ_BLOCK, LANES), lambda b, c: (b * nblk + c, 0)),
        out_shape=jax.ShapeDtypeStruct((m, LANES), F32),
        scratch_shapes=[pltpu.VMEM((1, LANES), F32)],
        compiler_params=_params(("parallel", "arbitrary")),
        name="neg_cum_logf",
    )(fg, b_f)


def _fox_attn_body(qtab_ref, ktab_ref, q_ref, k_ref, v_ref, nc_ref, o_ref,
                   qh_ref, vaug_ref, s_ref, p_ref, scale_ref, m_ref, acc_ref, *, nitems):
    nblk, tq = vaug_ref.shape[0], s_ref.shape[2]
    lane = lax.broadcasted_iota(jnp.int32, (1, LANES), 1)
    first_head = lane < HEAD_DIM

    ones_a = jnp.broadcast_to((lane == 0).astype(BF16), (tq, LANES))
    ones_b = jnp.broadcast_to((lane == 1).astype(BF16), (tq, LANES))
    for blk in range(nblk):
        rows = slice(blk * tq, (blk + 1) * tq)
        v = v_ref[rows, :]
        q = q_ref[rows, :]
        zero = jnp.zeros_like(v)
        vaug_ref[blk, 0:tq, 0:LANES] = jnp.where(first_head, v, zero)
        vaug_ref[blk, tq:2 * tq, 0:LANES] = jnp.where(first_head, zero, v)
        vaug_ref[blk, 0:tq, LANES:2 * LANES] = ones_a
        vaug_ref[blk, tq:2 * tq, LANES:2 * LANES] = ones_b
        qh_ref[blk, 0:tq, :] = jnp.where(first_head, q, zero) * (HEAD_DIM ** -0.5)
        qh_ref[blk, tq:2 * tq, :] = jnp.where(first_head, zero, q) * (HEAD_DIM ** -0.5)
    m_ref[...] = jnp.full_like(m_ref, -jnp.inf)
    acc_ref[...] = jnp.zeros_like(acc_ref)
    p_ref[1] = jnp.zeros(p_ref.shape[1:], p_ref.dtype)
    scale_ref[1] = jnp.ones(scale_ref.shape[1:], scale_ref.dtype)

    def scores(n, slot):
        qi = qtab_ref[n]
        kb = ktab_ref[n]
        rows = pl.ds(pl.multiple_of(kb * tq, tq), tq)
        s = lax.dot_general(qh_ref[qi], k_ref[rows, :], (((1,), (1,)), ((), ())),
                            preferred_element_type=F32)
        nc = nc_ref[kb]
        s_ref[slot, 0:tq, :] = s[0:tq] + nc[0:1, :]
        s_ref[slot, tq:2 * tq, :] = s[tq:2 * tq] + nc[1:2, :]

    def softmax(n, slot, diagonal):
        qi = qtab_ref[n]
        s = s_ref[slot]
        if diagonal:
            row = lax.broadcasted_iota(jnp.int32, (2 * tq, tq), 0)
            row = jnp.where(row >= tq, row - tq, row)
            col = lax.broadcasted_iota(jnp.int32, (2 * tq, tq), 1)
            s = jnp.where(col <= row, s, -jnp.inf)
        m_prev = m_ref[qi]
        m_next = jnp.maximum(m_prev, jnp.max(s, axis=-1, keepdims=True))
        m_ref[qi] = m_next
        alpha = jnp.exp(m_prev - m_next)
        p = jnp.exp(s - jnp.concatenate([m_next] * (tq // LANES), axis=1)).astype(BF16)
        p_ref[slot, :, 0:tq] = p[0:tq]
        p_ref[slot, :, tq:2 * tq] = p[tq:2 * tq]
        alpha_a = alpha[0:tq]
        alpha_b = alpha[tq:2 * tq]
        scale_ref[slot, :, 0:LANES] = jnp.where(first_head, alpha_a, alpha_b)
        scale_ref[slot, :, LANES:2 * LANES] = jnp.where(lane == 0, alpha_a, alpha_b)

    def accumulate(n, slot):
        qi = qtab_ref[n]
        pv = jnp.dot(p_ref[slot], vaug_ref[ktab_ref[n]], preferred_element_type=F32)
        acc_ref[qi] = scale_ref[slot] * acc_ref[qi] + pv

    def pair(j, diagonal0, diagonal1):
        n0 = 2 * j
        accumulate(jnp.maximum(n0 - 1, 0), 1)
        softmax(n0, 0, diagonal0)
        scores(n0 + 1, 1)
        accumulate(n0, 0)
        softmax(n0 + 1, 1, diagonal1)
        scores(n0 + 2, 0)

    def pairs(lo, hi, diagonal):
        def body(j, carry):
            pair(j, diagonal, diagonal)
            return carry
        lax.fori_loop(lo, hi, body, 0)

    scores(0, 0)
    pairs(0, nblk // 2, True)
    if nblk % 2:
        pair(nblk // 2, True, False)
    pairs((nblk + 1) // 2, nitems // 2, False)
    accumulate(nitems - 1, 1)

    for blk in range(nblk):
        acc = acc_ref[blk]
        sum_a = jnp.broadcast_to(acc[:, LANES:LANES + 1], (tq, LANES))
        sum_b = jnp.broadcast_to(acc[:, LANES + 1:LANES + 2], (tq, LANES))
        o_ref[blk * tq:(blk + 1) * tq, :] = (
            acc[:, 0:LANES] / jnp.where(first_head, sum_a, sum_b)).astype(o_ref.dtype)


def _fox_attn(qkv, negc, batch, lp):
    nslab, m, _ = qkv.shape
    nq = lp // SEQ_BLOCK
    items = [(qi, qi - d) for d in range(nq) for qi in range(d, nq)]
    nitems = len(items)
    assert nitems % 2 == 0
    items.append(items[-1])
    qtab = jnp.asarray([it[0] for it in items], jnp.int32)
    ktab = jnp.asarray([it[1] for it in items], jnp.int32)
    qkv_seq = qkv.reshape(nslab, batch, lp, LANES)
    seq_blk = (None, None, lp, LANES)
    grid_spec = pltpu.PrefetchScalarGridSpec(
        num_scalar_prefetch=2,
        grid=(batch, HEAD_PAIRS),
        in_specs=[
            pl.BlockSpec(seq_blk, lambda b, p, qt, kt: (p, b, 0, 0)),
            pl.BlockSpec(seq_blk, lambda b, p, qt, kt: (HEAD_PAIRS + p, b, 0, 0)),
            pl.BlockSpec(seq_blk, lambda b, p, qt, kt: (2 * HEAD_PAIRS + p, b, 0, 0)),
            pl.BlockSpec((None, None, nq, 2, SEQ_BLOCK), lambda b, p, qt, kt: (b, p, 0, 0, 0)),
        ],
        out_specs=pl.BlockSpec((lp, LANES), lambda b, p, qt, kt: (b, p)),
        scratch_shapes=[pltpu.VMEM((nq, 2 * SEQ_BLOCK, LANES), BF16),
                        pltpu.VMEM((nq, 2 * SEQ_BLOCK, 2 * LANES), BF16),
                        pltpu.VMEM((2, 2 * SEQ_BLOCK, SEQ_BLOCK), F32),
                        pltpu.VMEM((2, SEQ_BLOCK, 2 * SEQ_BLOCK), BF16),
                        pltpu.VMEM((2, SEQ_BLOCK, 2 * LANES), F32),
                        pltpu.VMEM((nq, 2 * SEQ_BLOCK, LANES), F32),
                        pltpu.VMEM((nq, SEQ_BLOCK, 2 * LANES), F32)],
    )
    return pl.pallas_call(
        functools.partial(_fox_attn_body, nitems=nitems),
        grid_spec=grid_spec,
        out_shape=jax.ShapeDtypeStruct((m, ATTN_WIDTH), BF16),
        compiler_params=_params(("parallel", "parallel")),
        name="fox_attn",
    )(qtab, ktab, qkv_seq, qkv_seq, qkv_seq, negc)


CONV_HALO = 16


def _gate_conv_body(x_ref, xh_ref, wb_ref, wc_ref, wx_ref, wf_ref, cw_ref, cb_ref, o_ref, f_ref, xcat_ref, z_ref):
    i = pl.program_id(0)
    j = pl.program_id(1)
    tm = x_ref.shape[0]

    @pl.when(j == 0)
    def _():
        xcat_ref[0:CONV_HALO, :] = xh_ref[...]
        xcat_ref[CONV_HALO:CONV_HALO + tm, :] = x_ref[...]
        f_ref[...] = jnp.dot(x_ref[...], wf_ref[...], preferred_element_type=F32)

    xcat = xcat_ref[...]
    gate_b = jnp.dot(xcat, wb_ref[...], preferred_element_type=F32)
    gate_c = jnp.dot(xcat, wc_ref[...], preferred_element_type=F32)
    conv_in = jnp.dot(xcat, wx_ref[...], preferred_element_type=F32)
    z_all = gate_c * conv_in
    z_ref[...] = z_all
    head = z_all[0:CONV_HALO]
    z_ref[0:CONV_HALO, :] = jnp.where(i > 0, head, jnp.zeros_like(head))
    z = z_all[CONV_HALO:CONV_HALO + tm]
    z1 = z_ref[CONV_HALO - 1:CONV_HALO - 1 + tm, :]
    z2 = z_ref[CONV_HALO - 2:CONV_HALO - 2 + tm, :]
    conv = cw_ref[0:1, :] * z2 + cw_ref[1:2, :] * z1 + cw_ref[2:3, :] * z + cb_ref[...]
    o_ref[...] = (gate_b[CONV_HALO:CONV_HALO + tm] * conv).astype(o_ref.dtype)


def _gate_conv(x, w, w_forget, layer, conv_w, conv_b, nch=512):
    m, k = x.shape
    channels = w.shape[2] // 3
    nblk = channels // nch
    halo_per_block = ROW_BLOCK // CONV_HALO

    def w_spec(kind):
        return pl.BlockSpec((None, k, nch), lambda i, j: (layer, 0, kind * nblk + j))

    return pl.pallas_call(
        _gate_conv_body,
        grid=(m // ROW_BLOCK, nblk),
        in_specs=[
            pl.BlockSpec((ROW_BLOCK, k), lambda i, j: (i, 0)),
            pl.BlockSpec((CONV_HALO, k), lambda i, j: (jnp.maximum(i * halo_per_block - 1, 0), 0)),
            w_spec(0), w_spec(1), w_spec(2),
            pl.BlockSpec((None, k, LANES), lambda i, j: (layer, 0, 0)),
            pl.BlockSpec((CONV_K, nch), lambda i, j: (0, j)),
            pl.BlockSpec((1, nch), lambda i, j: (0, j)),
        ],
        out_specs=[pl.BlockSpec((ROW_BLOCK, nch), lambda i, j: (i, j)),
                   pl.BlockSpec((ROW_BLOCK, LANES), lambda i, j: (i, 0))],
        out_shape=[jax.ShapeDtypeStruct((m, channels), BF16),
                   jax.ShapeDtypeStruct((m, LANES), F32)],
        scratch_shapes=[pltpu.VMEM((ROW_BLOCK + CONV_HALO, k), BF16),
                        pltpu.VMEM((ROW_BLOCK + CONV_HALO, nch), F32)],
        compiler_params=_params(("parallel", "arbitrary")),
        name="gate_conv",
    )(x, x, w, w, w, w_forget, conv_w, conv_b.reshape(1, -1))


def _residual_norm_store(m_ref, h_ref, g_ref, o_ref):
    nchunk, _, tn = m_ref.shape
    ss = None
    for c in range(nchunk):
        mc = m_ref[c]
        part = jnp.sum(mc * mc, axis=-1, keepdims=True)
        ss = part if ss is None else ss + part
    scale = lax.rsqrt(ss / (nchunk * tn) + NORM_EPS)
    for c in range(nchunk):
        cols = slice(c * tn, (c + 1) * tn)
        o_ref[:, cols] = h_ref[:, cols] + m_ref[c] * scale * g_ref[:, cols]


def _out_proj_body(xa_ref, xb_ref, w_ref, h_ref, g_ref, o_ref, m_ref):
    nchunk, _, tn = m_ref.shape
    ka = xa_ref.shape[1]
    xa = xa_ref[...]
    xb = xb_ref[...]
    for c in range(nchunk):
        cols = slice(c * tn, (c + 1) * tn)
        m_ref[c] = (jnp.dot(xa, w_ref[0:ka, cols].astype(BF16), preferred_element_type=F32)
                    + jnp.dot(xb, w_ref[ka:, cols].astype(BF16), preferred_element_type=F32))
    _residual_norm_store(m_ref, h_ref, g_ref, o_ref)


def _out_proj_residual(xa, xb, w, layer, h, g, tn=512):
    m, ka = xa.shape
    kb = xb.shape[1]
    n = w.shape[2]
    return pl.pallas_call(
        _out_proj_body,
        grid=(m // ROW_BLOCK,),
        in_specs=[
            pl.BlockSpec((ROW_BLOCK, ka), lambda i: (i, 0)),
            pl.BlockSpec((ROW_BLOCK, kb), lambda i: (i, 0)),
            pl.BlockSpec((None, ka + kb, n), lambda i: (layer, 0, 0), pipeline_mode=pl.Buffered(1)),
            pl.BlockSpec((ROW_BLOCK, n), lambda i: (i, 0)),
            pl.BlockSpec((1, n), lambda i: (0, 0)),
        ],
        out_specs=pl.BlockSpec((ROW_BLOCK, n), lambda i: (i, 0)),
        out_shape=jax.ShapeDtypeStruct((m, n), F32),
        scratch_shapes=[pltpu.VMEM((n // tn, ROW_BLOCK, tn), F32)],
        compiler_params=_params(("parallel",)),
        name="out_proj_residual",
    )(xa, xb, w, h, g.reshape(1, n))


def _glu_body(x_ref, w1_ref, w2_ref, h_ref, g_ref, o_ref, m_ref):
    nchunk, _, tn = m_ref.shape
    x = x_ref[...]
    for c in range(nchunk):
        cols = slice(c * tn, (c + 1) * tn)
        a = jnp.dot(x, w1_ref[:, cols], preferred_element_type=F32)
        b = jnp.dot(x, w2_ref[:, cols], preferred_element_type=F32)
        m_ref[c] = a * jax.nn.sigmoid(b)
    _residual_norm_store(m_ref, h_ref, g_ref, o_ref)


def _glu_residual(x, w1, w2, layer, h, g, tn=512):
    m, k = x.shape
    n = w1.shape[2]
    w_spec = pl.BlockSpec((None, k, n), lambda i: (layer, 0, 0), pipeline_mode=pl.Buffered(1))
    return pl.pallas_call(
        _glu_body,
        grid=(m // ROW_BLOCK,),
        in_specs=[
            pl.BlockSpec((ROW_BLOCK, k), lambda i: (i, 0)),
            w_spec, w_spec,
            pl.BlockSpec((ROW_BLOCK, n), lambda i: (i, 0)),
            pl.BlockSpec((1, n), lambda i: (0, 0)),
        ],
        out_specs=pl.BlockSpec((ROW_BLOCK, n), lambda i: (i, 0)),
        out_shape=jax.ShapeDtypeStruct((m, n), F32),
        scratch_shapes=[pltpu.VMEM((n // tn, ROW_BLOCK, tn), F32)],
        compiler_params=_params(("parallel",)),
        name="glu_residual",
    )(x, w1, w2, h, g.reshape(1, n))


def _ffn_body(h_ref, gin_ref, gout_ref, wg_ref, wu_ref, wd_ref, o_ref, *rest, emit_inv_rms):
    inv_ref = rest[0] if emit_inv_rms else None
    hn_ref, acc_ref = rest[-2:]
    j = pl.program_id(1)

    @pl.when(j == 0)
    def _():
        x = h_ref[...]
        ms = jnp.mean(x * x, axis=-1, keepdims=True)
        hn_ref[...] = (x * lax.rsqrt(ms + NORM_EPS) * gin_ref[...]).astype(hn_ref.dtype)
        acc_ref[...] = jnp.zeros_like(acc_ref)

    hn = hn_ref[...]
    gate = jnp.dot(hn, wg_ref[...], preferred_element_type=F32)
    up = jnp.dot(hn, wu_ref[...], preferred_element_type=F32)
    act = (gate * jax.nn.sigmoid(gate) * up).astype(BF16)
    acc_ref[...] += jnp.dot(act, wd_ref[...].astype(BF16), preferred_element_type=F32)

    @pl.when(j == pl.num_programs(1) - 1)
    def _():
        f = acc_ref[...]
        ms = jnp.mean(f * f, axis=-1, keepdims=True)
        out = h_ref[...] + f * lax.rsqrt(ms + NORM_EPS) * gout_ref[...]
        o_ref[...] = out
        if emit_inv_rms:
            ms_out = jnp.mean(out * out, axis=-1, keepdims=True)
            inv_ref[...] = jnp.broadcast_to(lax.rsqrt(ms_out + NORM_EPS), inv_ref.shape)


def _ffn_residual(h, g_in, g_out, w_gate, w_up, w_down, layer, emit_inv_rms, th=512):
    m, d = h.shape
    hidden = w_gate.shape[2]
    row_spec = pl.BlockSpec((ROW_BLOCK, d), lambda i, j: (i, 0))
    out_specs = [row_spec]
    out_shape = [jax.ShapeDtypeStruct((m, d), F32)]
    if emit_inv_rms:
        out_specs.append(pl.BlockSpec((ROW_BLOCK, LANES), lambda i, j: (i, 0)))
        out_shape.append(jax.ShapeDtypeStruct((m, LANES), F32))
    outs = pl.pallas_call(
        functools.partial(_ffn_body, emit_inv_rms=emit_inv_rms),
        grid=(m // ROW_BLOCK, hidden // th),
        in_specs=[
            row_spec,
            pl.BlockSpec((1, d), lambda i, j: (0, 0)),
            pl.BlockSpec((1, d), lambda i, j: (0, 0)),
            pl.BlockSpec((None, d, th), lambda i, j: (layer, 0, j)),
            pl.BlockSpec((None, d, th), lambda i, j: (layer, 0, j)),
            pl.BlockSpec((None, th, d), lambda i, j: (layer, j, 0)),
        ],
        out_specs=out_specs,
        out_shape=out_shape,
        scratch_shapes=[pltpu.VMEM((ROW_BLOCK, d), BF16),
                        pltpu.VMEM((ROW_BLOCK, d), F32)],
        compiler_params=_params(("parallel", "arbitrary")),
        name="ffn_residual",
    )(h, g_in.reshape(1, d), g_out.reshape(1, d), w_gate, w_up, w_down)
    return tuple(outs) if emit_inv_rms else outs[0]


def _lane_repeat(x, reps, exact_f32):
    n = x.shape[1]
    row = lax.broadcasted_iota(jnp.int32, (n, n * reps), 0)
    col = lax.broadcasted_iota(jnp.int32, (n, n * reps), 1)
    sel = (jnp.bitwise_and(col, n - 1) == row).astype(BF16)
    hi = x.astype(BF16)
    out = jnp.dot(hi, sel, preferred_element_type=F32)
    if exact_f32:
        rem = x - hi.astype(F32)
        mid = rem.astype(BF16)
        lo = (rem - mid.astype(F32)).astype(BF16)
        out = out + jnp.dot(mid, sel, preferred_element_type=F32) + jnp.dot(lo, sel, preferred_element_type=F32)
    return out


def _diag_block_mask(shape, row_shift, col_shift):
    row = lax.broadcasted_iota(jnp.int32, shape, 0)
    col = lax.broadcasted_iota(jnp.int32, shape, 1)
    return jnp.right_shift(row, row_shift) == jnp.right_shift(col, col_shift)


def _s5_operands_body(are_ref, aim_ref, ls_ref, bre_ref, bim_ref, cre_ref, cim_ref,
                      lbr_ref, lbi_ref, bmat_ref, cmr_ref, cmi_ref):
    lam_re = jnp.minimum(are_ref[...], -S5_MIN_DECAY)
    lam_im = aim_ref[...]
    delta = jnp.exp(ls_ref[...])
    mag = jnp.exp(lam_re * delta)
    ang = lam_im * delta
    lb_re = mag * jnp.cos(ang)
    lb_im = mag * jnp.sin(ang)
    den = lam_re * lam_re + lam_im * lam_im
    nr = lb_re - 1.0
    ni = lb_im
    coef_re = (nr * lam_re + ni * lam_im) / den
    coef_im = (ni * lam_re - nr * lam_im) / den
    lbr_ref[...] = lb_re
    lbi_ref[...] = lb_im

    br = _lane_repeat(bre_ref[...], S5_SET_GROUPS, True)
    bi = _lane_repeat(bim_ref[...], S5_SET_GROUPS, True)
    in_mask = _diag_block_mask(br.shape, S5_GROUP.bit_length() - 1, S5_STATE.bit_length() - 1)
    zero = jnp.zeros_like(br)
    bmat_ref[:, 0:S5_SET_STATE] = jnp.where(in_mask, coef_re * br - coef_im * bi, zero).astype(bmat_ref.dtype)
    bmat_ref[:, S5_SET_STATE:2 * S5_SET_STATE] = jnp.where(
        in_mask, coef_re * bi + coef_im * br, zero).astype(bmat_ref.dtype)

    cr = _lane_repeat(cre_ref[...], S5_SET_GROUPS, False)
    ci = _lane_repeat(cim_ref[...], S5_SET_GROUPS, False)
    out_mask = _diag_block_mask(cr.shape, S5_STATE.bit_length() - 1, S5_GROUP.bit_length() - 1)
    cmr_ref[...] = jnp.where(out_mask, cr, jnp.zeros_like(cr)).astype(cmr_ref.dtype)
    cmi_ref[...] = jnp.where(out_mask, ci, jnp.zeros_like(ci)).astype(cmi_ref.dtype)


def _s5_operands(a_re, a_im, log_step, b_re, b_im, c_re, c_im):
    groups, state = a_re.shape
    nset = groups // S5_SET_GROUPS
    lane_row = lambda a: a.reshape(nset, 1, S5_SET_STATE)
    step_row = lane_row(jnp.broadcast_to(log_step[:, None], (groups, state)))
    b_rows = lambda b: jnp.transpose(b, (0, 2, 1)).reshape(nset, S5_SET_IN, state)
    c_rows = lambda c: jnp.transpose(c, (0, 2, 1)).reshape(nset, S5_SET_STATE, S5_GROUP)
    row_spec = pl.BlockSpec((None, 1, S5_SET_STATE), lambda s: (s, 0, 0))
    b_spec = pl.BlockSpec((None, S5_SET_IN, state), lambda s: (s, 0, 0))
    c_spec = pl.BlockSpec((None, S5_SET_STATE, S5_GROUP), lambda s: (s, 0, 0))
    cm_spec = pl.BlockSpec((None, S5_SET_STATE, S5_SET_IN), lambda s: (s, 0, 0))
    lb_re, lb_im, bmat, cmr, cmi = pl.pallas_call(
        _s5_operands_body,
        grid=(nset,),
        in_specs=[row_spec, row_spec, row_spec, b_spec, b_spec, c_spec, c_spec],
        out_specs=[row_spec, row_spec,
                   pl.BlockSpec((None, S5_SET_IN, 2 * S5_SET_STATE), lambda s: (s, 0, 0)), cm_spec, cm_spec],
        out_shape=[jax.ShapeDtypeStruct((nset, 1, S5_SET_STATE), F32)] * 2
        + [jax.ShapeDtypeStruct((nset, S5_SET_IN, 2 * S5_SET_STATE), BF16)]
        + [jax.ShapeDtypeStruct((nset, S5_SET_STATE, S5_SET_IN), BF16)] * 2,
        compiler_params=_params(("parallel",)),
        name="s5_operands",
    )(lane_row(a_re), lane_row(a_im), step_row, b_rows(b_re), b_rows(b_im), c_rows(c_re), c_rows(c_im))
    lam_r = lb_re.reshape(nset, STATE_TILES, LANES)
    lam_i = lb_im.reshape(nset, STATE_TILES, LANES)
    return bmat, cmr, cmi, lam_r, lam_i


def _s5_scan_body(h_ref, inv_ref, g_ref, bmat_ref, cre_ref, cim_ref, lr_ref, li_ref, d_ref, o_ref, slab_ref,
                  carry_ref):
    c = pl.program_id(1)
    nb, ts, _ = h_ref.shape
    nsets = bmat_ref.shape[0]
    chains = [(ss, b) for ss in range(nsets) for b in range(nb)]

    @pl.when(c == 0)
    def _():
        carry_ref[...] = jnp.zeros_like(carry_ref)

    def slab_base(chain, tile):
        return (chain * 2 * STATE_TILES + tile) * SLAB_PITCH

    def set_cols(ss):
        return slice(ss * S5_SET_IN, (ss + 1) * S5_SET_IN)

    def normed(ss, b):
        inv = jnp.concatenate([inv_ref[b]] * (S5_SET_IN // LANES), axis=1)
        return h_ref[b, :, set_cols(ss)] * inv * g_ref[:, set_cols(ss)]

    for chain, (ss, b) in enumerate(chains):
        bu = jnp.dot(normed(ss, b).astype(BF16), bmat_ref[ss], preferred_element_type=F32)
        for tile in range(2 * STATE_TILES):
            slab_ref[pl.ds(slab_base(chain, tile), ts), :] = bu[:, tile * LANES:(tile + 1) * LANES]

    lam = [(lr_ref[ss], li_ref[ss]) for ss in range(nsets)]

    def step(t, state):
        new_state = []
        for chain, (ss, b) in enumerate(chains):
            lam_r, lam_i = lam[ss]
            xr, xi = state[2 * chain], state[2 * chain + 1]
            re_rows = pl.ds(slab_base(chain, 0) + t, STATE_TILES, stride=SLAB_PITCH)
            im_rows = pl.ds(slab_base(chain, STATE_TILES) + t, STATE_TILES, stride=SLAB_PITCH)
            nxr = lam_r * xr - lam_i * xi + slab_ref[re_rows, :]
            nxi = lam_r * xi + lam_i * xr + slab_ref[im_rows, :]
            slab_ref[re_rows, :] = nxr
            slab_ref[im_rows, :] = nxi
            new_state += [nxr, nxi]
        return tuple(new_state)

    init = tuple(carry_ref[s] for s in range(2 * len(chains)))
    final = lax.fori_loop(0, ts, step, init, unroll=8)
    for s in range(2 * len(chains)):
        carry_ref[s] = final[s]

    for chain, (ss, b) in enumerate(chains):
        xr = jnp.concatenate(
            [slab_ref[pl.ds(slab_base(chain, tile), ts), :] for tile in range(STATE_TILES)], axis=1)
        xi = jnp.concatenate(
            [slab_ref[pl.ds(slab_base(chain, STATE_TILES + tile), ts), :] for tile in range(STATE_TILES)], axis=1)
        y = (jnp.dot(xr.astype(BF16), cre_ref[ss], preferred_element_type=F32)
             - jnp.dot(xi.astype(BF16), cim_ref[ss], preferred_element_type=F32))
        y = y + d_ref[:, set_cols(ss)] * normed(ss, b)
        o_ref[b, :, set_cols(ss)] = jax.nn.gelu(y).astype(o_ref.dtype)


def _s5_scan(h3, inv3, g, bmat, cre, cim, lam_r, lam_i, d_skip):
    nb, lp, d = h3.shape
    nset = bmat.shape[0]
    per = S5_SETS_PER_STEP
    cols = per * S5_SET_IN
    chains = per * nb
    return pl.pallas_call(
        _s5_scan_body,
        grid=(nset // per, lp // SEQ_BLOCK),
        in_specs=[
            pl.BlockSpec((nb, SEQ_BLOCK, cols), lambda s, c: (0, c, s)),
            pl.BlockSpec((nb, SEQ_BLOCK, LANES), lambda s, c: (0, c, 0)),
            pl.BlockSpec((1, cols), lambda s, c: (0, s)),
            pl.BlockSpec((per, S5_SET_IN, 2 * S5_SET_STATE), lambda s, c: (s, 0, 0)),
            pl.BlockSpec((per, S5_SET_STATE, S5_SET_IN), lambda s, c: (s, 0, 0)),
            pl.BlockSpec((per, S5_SET_STATE, S5_SET_IN), lambda s, c: (s, 0, 0)),
            pl.BlockSpec((per, STATE_TILES, LANES), lambda s, c: (s, 0, 0)),
            pl.BlockSpec((per, STATE_TILES, LANES), lambda s, c: (s, 0, 0)),
            pl.BlockSpec((1, cols), lambda s, c: (0, s)),
        ],
        out_specs=pl.BlockSpec((nb, SEQ_BLOCK, cols), lambda s, c: (0, c, s)),
        out_shape=jax.ShapeDtypeStruct((nb, lp, d), BF16),
        scratch_shapes=[pltpu.VMEM((chains * 2 * STATE_TILES * SLAB_PITCH, LANES), F32),
                        pltpu.VMEM((2 * chains, STATE_TILES, LANES), F32)],
        compiler_params=_params(("parallel", "arbitrary")),
        name="s5_scan",
    )(h3, inv3, g.reshape(1, d), bmat, cre, cim, lam_r, lam_i, d_skip.reshape(1, d))


def _attn_conv_layer(h, g, w_qkv, w_fg, w_gates, w_o, layer, b_f, conv_w, conv_b, batch, lp, pad):
    u = _rmsnorm(h, g[0], BF16)
    qkv = _matmul_slabs(u, w_qkv, layer, 0, w_qkv.shape[2], BF16, 768, "qkv_proj")
    conv, fg = _gate_conv(u, w_gates, w_fg, layer, conv_w, conv_b)
    b_pad = jnp.pad(b_f, (0, LANES - ATTN_HEADS)).reshape(1, LANES)
    negc = _neg_cum_logf(fg, b_pad, batch, lp, pad)
    negc = jnp.transpose(negc[:, :ATTN_HEADS].reshape(batch, lp // SEQ_BLOCK, SEQ_BLOCK, HEAD_PAIRS, 2),
                         (0, 3, 1, 4, 2))
    attn = _fox_attn(qkv, negc, batch, lp)
    return _out_proj_residual(attn, conv, w_o, layer, h, g[1])


def _s5_layer(h, inv_rms, g, a_re, a_im, log_step, b_re, b_im, c_re, c_im, d_skip, w_glu1, w_glu2, layer, batch, lp):
    m, d = h.shape
    operands = _s5_operands(a_re, a_im, log_step, b_re, b_im, c_re, c_im)
    act = _s5_scan(h.reshape(batch, lp, d), inv_rms.reshape(batch, lp, LANES), g[0], *operands, d_skip)
    return _glu_residual(act.reshape(m, d), w_glu1, w_glu2, layer, h, g[1])


def _projection_weights(w_in):
    qkv_w = 3 * ATTN_WIDTH
    w_all = w_in.astype(BF16)
    w_fg = jnp.pad(w_all[..., qkv_w:qkv_w + ATTN_HEADS], ((0, 0), (0, 0), (0, LANES - ATTN_HEADS)))
    return w_all[..., :qkv_w], w_fg, w_all[..., qkv_w + ATTN_HEADS:]


def kernel(x, meta_tokens, norm_g, ab_w_in, ab_b_f, ab_conv_w, ab_conv_b, ab_w_o, s5_a_re, s5_a_im, s5_log_step, s5_b_re, s5_b_im, s5_c_re, s5_c_im, s5_d, s5_w_glu1, s5_w_glu2, ffn_w_gate, ffn_w_up, ffn_w_down):
    batch, seq, d = x.shape
    depth = norm_g.shape[0]
    tokens = N_META + seq
    lp = pl.cdiv(tokens, SEQ_BLOCK) * SEQ_BLOCK
    assert (batch * lp) % ROW_BLOCK == 0
    pad = lp - tokens
    meta = jnp.broadcast_to(meta_tokens.astype(x.dtype)[None], (batch, N_META, d))
    h = jnp.concatenate([jnp.zeros((batch, pad, d), x.dtype), meta, x], axis=1).reshape(batch * lp, d)
    w_qkv, w_fg, w_gates = _projection_weights(ab_w_in)
    w_o = ab_w_o
    w_glu1 = s5_w_glu1.astype(BF16)
    w_glu2 = s5_w_glu2.astype(BF16)
    w_gate = ffn_w_gate.astype(BF16)
    w_up = ffn_w_up.astype(BF16)
    w_down = ffn_w_down
    inv_rms = None
    for i in range(depth):
        g = norm_g[i]
        j = i // 2
        if i % 2 == 0:
            h = _attn_conv_layer(h, g, w_qkv, w_fg, w_gates, w_o, j, ab_b_f[j], ab_conv_w[j], ab_conv_b[j],
                                 batch, lp, pad)
        else:
            h = _s5_layer(h, inv_rms, g, s5_a_re[j], s5_a_im[j], s5_log_step[j], s5_b_re[j], s5_b_im[j],
                          s5_c_re[j], s5_c_im[j], s5_d[j], w_glu1, w_glu2, j, batch, lp)
        next_is_s5 = i + 1 < depth and (i + 1) % 2 == 1
        out = _ffn_residual(h, g[2], g[3], w_gate, w_up, w_down, i, next_is_s5)
        h, inv_rms = out if next_is_s5 else (out, None)
    return h.reshape(batch, lp, d)[:, pad + N_META:]
```
